```python
import math
import jax
import jax.numpy as jnp
from jax import lax
import numpy as np

D_MODEL = 2048
BATCH = 1
SEQ = 16384
DEPTH = 4
DEC_BATCH = 16
DEC_SEQ = 32
PAST_LEN = 2048

CHUNK = 64
QBLK = 128
N_BRANCH = 4
BR_W = 512
HD = 128
CONV_W = 3
B_HEADS = 4
IDX_HEADS = 4
IDX_HD = 64
TOPK_MAX = 256
IDX_SCALE = (IDX_HEADS * IDX_HD) ** -0.5
C_HEADS = 4
D_HEADS = 4
D_QK = 64
D_V = 128
ROPE_THETA = 500000.0
D_FF = 4 * D_MODEL
LN_EPS = 1e-5
RMS_EPS = 1e-5
ALPHA = (2 * DEPTH) ** 0.25
BETA = (8 * DEPTH) ** -0.25
FORGET_BIAS_MEAN = 3.0
DSA_ROW = 2 * HD + IDX_HD
FOX_ROW = 2 * HD + 1
DIFF_ROW = 2 * D_QK + D_V
SPLIT_SIZES = (BR_W, BR_W, BR_W,
               B_HEADS * HD, HD, HD, IDX_HEADS * IDX_HD, IDX_HD, IDX_HEADS,
               C_HEADS * HD, C_HEADS * HD, C_HEADS * HD, C_HEADS,
               D_HEADS * 2 * D_QK, D_HEADS * 2 * D_QK, D_HEADS * D_V,
               N_BRANCH * D_MODEL)
SPLIT_POINTS = tuple(int(v) for v in np.cumsum(SPLIT_SIZES)[:-1])
P_IN = int(sum(SPLIT_SIZES))

kernel_name = 'hybrid_stream_encoder_step'


def _rope(x, pos):
    rot = x.shape[-1] // 4
    half = rot // 2
    freqs = ROPE_THETA ** (-2.0 * jnp.arange(half, dtype=jnp.float32) / rot)
    ang = pos.astype(jnp.float32)[:, None] * freqs[None, :]
    cos = jnp.cos(ang)[None, :, None, :]
    sin = jnp.sin(ang)[None, :, None, :]
    xf = x.astype(jnp.float32)
    x1, x2 = xf[..., :half], xf[..., half:rot]
    out = jnp.concatenate([x1 * cos - x2 * sin, x1 * sin + x2 * cos, xf[..., rot:]], axis=-1)
    return out.astype(x.dtype)


def _layernorm(x, g, b):
    xf = x.astype(jnp.float32)
    mu = jnp.mean(xf, axis=-1, keepdims=True)
    var = jnp.mean(jnp.square(xf - mu), axis=-1, keepdims=True)
    y = (xf - mu) * lax.rsqrt(var + LN_EPS) * g.astype(jnp.float32) + b.astype(jnp.float32)
    return y.astype(x.dtype)


def _sweep(fn, *arrs):
    b, t = arrs[0].shape[:2]
    qb = QBLK if t % QBLK == 0 else t
    n = t // qb
    blocks = tuple(jnp.moveaxis(a.reshape((b, n, qb) + a.shape[2:]), 1, 0) for a in arrs)
    starts = jnp.arange(n, dtype=jnp.int32) * qb
    out = lax.map(lambda args: fn(args[0], *args[1:]), (starts,) + blocks)
    out = jnp.moveaxis(out, 0, 1)
    return out.reshape((b, t) + out.shape[3:])


def _short_conv(gate_b, gate_c, h, conv_w, past):
    v = gate_c * h
    buf = jnp.concatenate([past, v], axis=1)
    t = v.shape[1]
    y = buf[:, 0:t] * conv_w[0]
    for j in range(1, CONV_W):
        y = y + buf[:, j:j + t] * conv_w[j]
    return gate_b * y, buf[:, buf.shape[1] - (CONV_W - 1):]


def _dsa_attention(q, k, v, iq, ik, iw, past, past_len):
    b, t = q.shape[:2]
    pos = past_len + jnp.arange(t, dtype=jnp.int32)
    q = _rope(q.reshape(b, t, B_HEADS, HD), pos)
    k = _rope(k[:, :, None, :], pos)[:, :, 0]
    iq = _rope(iq.reshape(b, t, IDX_HEADS, IDX_HD), pos)
    ik = _rope(ik[:, :, None, :], pos)[:, :, 0]
    new_rows = jnp.concatenate([k, v, ik], axis=-1)
    rows = jnp.concatenate([past, new_rows], axis=1)
    keys, vals = rows[..., :HD], rows[..., HD:2 * HD]
    ikeys = rows[..., 2 * HD:].astype(jnp.float32)
    n_keys = rows.shape[1]
    k_sel = min(TOPK_MAX, n_keys // 4)
    key_chunk = jnp.arange(n_keys, dtype=jnp.int32) // CHUNK
    gather = jax.vmap(lambda a, i: a[i])

    def block(start, qb_, iqb, iwb):
        qpos = past_len + start + jnp.arange(qb_.shape[1], dtype=jnp.int32)
        adm = key_chunk[None, :] <= (qpos // CHUNK)[:, None]
        rel = jax.nn.relu(jnp.einsum('bqhd,bsd->bqhs', iqb.astype(jnp.float32), ikeys))
        score = jnp.einsum('bqhs,bqh->bqs', rel, iwb.astype(jnp.float32)) * IDX_SCALE
        score = jnp.where(adm[None], score, -jnp.inf)
        top, idx = lax.top_k(score, k_sel)
        valid = jnp.isfinite(top)
        ks = gather(keys, idx).astype(jnp.float32)
        vs = gather(vals, idx).astype(jnp.float32)
        logits = jnp.einsum('bqhd,bqkd->bhqk', qb_.astype(jnp.float32), ks) * HD ** -0.5
        logits = jnp.where(valid[:, None], logits, -jnp.inf)
        p = jax.nn.softmax(logits, axis=-1)
        return jnp.einsum('bhqk,bqkd->bqhd', p, vs).astype(qb_.dtype)

    o = _sweep(block, q, iq, iw)
    return o.reshape(b, t, B_HEADS * HD), new_rows


def _forgetting_attention(q, k, v, f_logit, f_bias, past, past_len):
    b, t = q.shape[:2]
    q = q.reshape(b, t, C_HEADS, HD)
    k = k.reshape(b, t, C_HEADS, HD)
    v = v.reshape(b, t, C_HEADS, HD)
    logf = jax.nn.log_sigmoid(f_logit.astype(jnp.float32) + f_bias.astype(jnp.float32))
    new_rows = jnp.concatenate([k, v, logf[..., None].astype(k.dtype)], axis=-1)
    rows = jnp.concatenate([past, new_rows], axis=1)
    keys = rows[..., :HD].astype(jnp.float32)
    vals = rows[..., HD:2 * HD].astype(jnp.float32)
    cum = jnp.cumsum(rows[..., 2 * HD].astype(jnp.float32), axis=1)
    cum_k = jnp.transpose(cum, (0, 2, 1))[:, :, None, :]
    cum_q = cum[:, past_len:]
    key_pos = jnp.arange(rows.shape[1], dtype=jnp.int32)

    def block(start, qb_, cqb):
        qpos = past_len + start + jnp.arange(qb_.shape[1], dtype=jnp.int32)
        logits = jnp.einsum('bqhd,bshd->bhqs', qb_.astype(jnp.float32), keys) * HD ** -0.5
        logits = logits + jnp.transpose(cqb, (0, 2, 1))[..., None] - cum_k
        logits = jnp.where(key_pos[None, :] <= qpos[:, None], logits, -jnp.inf)
        p = jax.nn.softmax(logits, axis=-1)
        return jnp.einsum('bhqs,bshd->bqhd', p, vals).astype(qb_.dtype)

    o = _sweep(block, q, cum_q)
    return o.reshape(b, t, C_HEADS * HD), new_rows


def _diff_attention(q, k, v, diff_lambda, diff_gain, lam_init, past, past_len):
    b, t = q.shape[:2]
    pos = past_len + jnp.arange(t, dtype=jnp.int32)
    q = _rope(q.reshape(b, t, D_HEADS * 2, D_QK), pos).reshape(b, t, D_HEADS, 2, D_QK)
    k = _rope(k.reshape(b, t, D_HEADS * 2, D_QK), pos).reshape(b, t, D_HEADS, 2 * D_QK)
    v = v.reshape(b, t, D_HEADS, D_V)
    new_rows = jnp.concatenate([k, v], axis=-1)
    rows = jnp.concatenate([past, new_rows], axis=1)
    n_keys = rows.shape[1]
    keys = rows[..., :2 * D_QK].astype(jnp.float32).reshape(b, n_keys, D_HEADS, 2, D_QK)
    vals = rows[..., 2 * D_QK:].astype(jnp.float32)
    lam = diff_lambda.astype(jnp.float32)
    lmb = jnp.exp(jnp.sum(lam[0] * lam[1])) - jnp.exp(jnp.sum(lam[2] * lam[3])) + lam_init
    key_chunk = jnp.arange(n_keys, dtype=jnp.int32) // CHUNK
    gain = diff_gain.astype(jnp.float32)

    def block(start, qb_):
        qpos = past_len + start + jnp.arange(qb_.shape[1], dtype=jnp.int32)
        mask = key_chunk[None, :] <= (qpos // CHUNK)[:, None]
        logits = jnp.einsum('bqhmd,bshmd->bhmqs', qb_.astype(jnp.float32), keys) * D_QK ** -0.5
        logits = jnp.where(mask, logits, -jnp.inf)
        p = jax.nn.softmax(logits, axis=-1)
        attn = p[:, :, 0] - lmb * p[:, :, 1]
        o = jnp.einsum('bhqs,bshd->bqhd', attn, vals)
        o = o * lax.rsqrt(jnp.mean(jnp.square(o), axis=-1, keepdims=True) + RMS_EPS)
        return (o * gain * (1.0 - lam_init)).astype(qb_.dtype)

    o = _sweep(block, q)
    return o.reshape(b, t, D_HEADS * D_V), new_rows


def _layer(x, past_len, conv_state, dsa_rows, fox_rows, diff_rows, w_in, conv_w, fox_fbias,
           diff_lambda, diff_gain, lam_init, w_branch, w_o, ln1_g, ln1_b, w_up, w_down, ln2_g, ln2_b):
    u = jnp.einsum('btd,dp->btp', x, w_in)
    (a_b, a_c, a_h, b_q, b_k, b_v, b_iq, b_ik, b_iw,
     c_q, c_k, c_v, c_f, d_q, d_k, d_v, gate) = jnp.split(u, SPLIT_POINTS, axis=-1)
    y_a, n_conv = _short_conv(a_b, a_c, a_h, conv_w, conv_state)
    y_b, n_dsa = _dsa_attention(b_q, b_k, b_v, b_iq, b_ik, b_iw, dsa_rows, past_len)
    y_c, n_fox = _forgetting_attention(c_q, c_k, c_v, c_f, fox_fbias, fox_rows, past_len)
    y_d, n_diff = _diff_attention(d_q, d_k, d_v, diff_lambda, diff_gain, lam_init, diff_rows, past_len)
    ys = (y_a, y_b, y_c, y_d)
    merged = 0.0
    for m in range(N_BRANCH):
        g = jax.nn.sigmoid(gate[..., m * D_MODEL:(m + 1) * D_MODEL])
        merged = merged + g * jnp.einsum('btc,cd->btd', ys[m], w_branch[m])
    h = jnp.einsum('btd,de->bte', merged, w_o)
    x = _layernorm(ALPHA * x + h, ln1_g, ln1_b)
    f = jnp.einsum('btf,fd->btd', jnp.square(jax.nn.relu(jnp.einsum('btd,df->btf', x, w_up))), w_down)
    x = _layernorm(ALPHA * x + f, ln2_g, ln2_b)
    return x, n_conv, n_dsa, n_fox, n_diff


def _trunk(x, past_len, conv_state, dsa_rows, fox_rows, diff_rows, w_in, conv_w, fox_fbias,
           diff_lambda, diff_gain, w_branch, w_o, ln1_g, ln1_b, w_up, w_down, ln2_g, ln2_b):
    conv_out, dsa_out, fox_out, diff_out = [], [], [], []
    for l in range(DEPTH):
        lam_init = 0.8 - 0.6 * math.exp(-0.3 * l)
        x, n_conv, n_dsa, n_fox, n_diff = _layer(
            x, past_len, conv_state[l], dsa_rows[l], fox_rows[l], diff_rows[l], w_in[l], conv_w[l],
            fox_fbias[l], diff_lambda[l], diff_gain[l], lam_init, w_branch[l], w_o[l], ln1_g[l], ln1_b[l],
            w_up[l], w_down[l], ln2_g[l], ln2_b[l])
        conv_out.append(n_conv)
        dsa_out.append(n_dsa)
        fox_out.append(n_fox)
        diff_out.append(n_diff)
    return x, jnp.stack(conv_out), jnp.stack(dsa_out), jnp.stack(fox_out), jnp.stack(diff_out)


def setup_inputs(seed: int = 0) -> dict:
    key = jax.random.key(seed)
    ks = jax.random.split(key, 20)
    nrm = jax.random.normal
    f32 = jnp.float32
    x_prompt = nrm(ks[0], (BATCH, SEQ, D_MODEL), f32)
    x_sample = nrm(ks[1], (DEC_BATCH, DEC_SEQ, D_MODEL), f32)
    state_conv = nrm(ks[2], (DEPTH, DEC_BATCH, CONV_W - 1, BR_W), f32)
    cache_dsa = nrm(ks[3], (DEPTH, DEC_BATCH, PAST_LEN, DSA_ROW), f32)
    fox_kv = nrm(ks[4], (DEPTH, DEC_BATCH, PAST_LEN, C_HEADS, 2 * HD), f32)
    fox_lf = jax.nn.log_sigmoid(FORGET_BIAS_MEAN + nrm(ks[5], (DEPTH, DEC_BATCH, PAST_LEN, C_HEADS, 1), f32))
    cache_fox = jnp.concatenate([fox_kv, fox_lf], axis=-1)
    cache_diff = nrm(ks[6], (DEPTH, DEC_BATCH, PAST_LEN, D_HEADS, DIFF_ROW), f32)
    w_in = nrm(ks[7], (DEPTH, D_MODEL, P_IN), f32) * D_MODEL ** -0.5
    conv_w = nrm(ks[8], (DEPTH, CONV_W, BR_W), f32) * CONV_W ** -0.5
    fox_fbias = FORGET_BIAS_MEAN + 0.5 * nrm(ks[9], (DEPTH, C_HEADS), f32)
    diff_lambda = 0.1 * nrm(ks[10], (DEPTH, 4, D_QK), f32)
    diff_gain = 1.0 + 0.02 * nrm(ks[11], (DEPTH, D_V), f32)
    w_branch = nrm(ks[12], (DEPTH, N_BRANCH, BR_W, D_MODEL), f32) * (BR_W ** -0.5 * BETA)
    w_o = nrm(ks[13], (DEPTH, D_MODEL, D_MODEL), f32) * (D_MODEL ** -0.5 * BETA)
    ln1_g = 1.0 + 0.02 * nrm(ks[14], (DEPTH, D_MODEL), f32)
    ln1_b = 0.02 * nrm(ks[15], (DEPTH, D_MODEL), f32)
    w_up = nrm(ks[16], (DEPTH, D_MODEL, D_FF), f32) * D_MODEL ** -0.5
    w_down = nrm(ks[17], (DEPTH, D_FF, D_MODEL), f32) * (D_FF ** -0.5 * BETA)
    ln2_g = 1.0 + 0.02 * nrm(ks[18], (DEPTH, D_MODEL), f32)
    ln2_b = 0.02 * nrm(ks[19], (DEPTH, D_MODEL), f32)
    return {'x_prompt': x_prompt, 'x_sample': x_sample, 'state_conv': state_conv,
            'cache_dsa': cache_dsa, 'cache_fox': cache_fox, 'cache_diff': cache_diff,
            'w_in': w_in, 'conv_w': conv_w, 'fox_fbias': fox_fbias, 'diff_lambda': diff_lambda,
            'diff_gain': diff_gain, 'w_branch': w_branch, 'w_o': w_o, 'ln1_g': ln1_g, 'ln1_b': ln1_b,
            'w_up': w_up, 'w_down': w_down, 'ln2_g': ln2_g, 'ln2_b': ln2_b}


def reference(x_prompt, x_sample, state_conv, cache_dsa, cache_fox, cache_diff, w_in, conv_w, fox_fbias,
              diff_lambda, diff_gain, w_branch, w_o, ln1_g, ln1_b, w_up, w_down, ln2_g, ln2_b):
    b = x_prompt.shape[0]
    dt = x_prompt.dtype
    conv0 = jnp.zeros((DEPTH, b, CONV_W - 1, BR_W), dt)
    dsa0 = jnp.zeros((DEPTH, b, 0, DSA_ROW), dt)
    fox0 = jnp.zeros((DEPTH, b, 0, C_HEADS, FOX_ROW), dt)
    diff0 = jnp.zeros((DEPTH, b, 0, D_HEADS, DIFF_ROW), dt)
    y_prompt, conv_p, dsa_p, fox_p, diff_p = _trunk(
        x_prompt, 0, conv0, dsa0, fox0, diff0, w_in, conv_w, fox_fbias, diff_lambda, diff_gain,
        w_branch, w_o, ln1_g, ln1_b, w_up, w_down, ln2_g, ln2_b)
    y_sample, conv_s, dsa_s, fox_s, diff_s = _trunk(
        x_sample, cache_dsa.shape[2], state_conv, cache_dsa, cache_fox, cache_diff, w_in, conv_w,
        fox_fbias, diff_lambda, diff_gain, w_branch, w_o, ln1_g, ln1_b, w_up, w_down, ln2_g, ln2_b)
    return (y_prompt, y_sample, conv_p, dsa_p, fox_p, diff_p, conv_s, dsa_s, fox_s, diff_s)
```

```python
import functools

import numpy as np
import jax
import jax.numpy as jnp
from jax import lax
from jax.experimental import pallas as pl
from jax.experimental.pallas import tpu as pltpu

CHUNK = 64
N_BRANCH = 4
BR_W = 512
HD = 128
CONV_W = 3
B_HEADS = 4
IDX_HEADS = 4
IDX_HD = 64
TOPK_MAX = 256
IDX_SCALE = (IDX_HEADS * IDX_HD) ** -0.5
C_HEADS = 4
D_HEADS = 4
D_QK = 64
D_V = 128
ROPE_THETA = 500000.0
LN_EPS = 1e-5
RMS_EPS = 1e-5

LANE = 128
VMEM_LIMIT = 56 * 1024 * 1024
NEG = -1e30
INT_MIN = np.int32(-2 ** 31)
F32 = jnp.float32
BF16 = jnp.bfloat16

R_BQ, R_BIQ, R_BK, R_BIK, R_DQ, R_DK, R_BLOCKS = 0, 4, 6, 7, 8, 12, 16
P_CQ, P_CK, P_CV, P_DV, P_BV, P_SMALL, P_BLOCKS = 0, 4, 8, 12, 16, 17, 18
IW_LANE, CF_LANE = 0, IDX_HEADS
R_KINDS = ("r128",) * 4 + ("r64",) * 2 + ("r128", "r64") + ("r64",) * 8
R_SCALES = (HD ** -0.5,) * 4 + (1.0,) * 4 + (D_QK ** -0.5,) * 4 + (1.0,) * 4
P_KINDS = (None,) * P_BLOCKS
P_SCALES = (HD ** -0.5,) * 4 + (1.0,) * 14


def _tile(n, target, align=8):
    if n <= target:
        return n
    for t in range(target - target % align, 0, -align):
        if n % t == 0:
            return t
    return n


def _params(n_grid):
    return pltpu.CompilerParams(dimension_semantics=("arbitrary",) * n_grid,
                                vmem_limit_bytes=VMEM_LIMIT)


def _mm_body(x_ref, w_ref, *rest, kinds, scales, has_tab, sigmoid, want_f32, want_b16):
    tab_ref = rest[0] if has_tab else None
    outs = rest[1:] if has_tab else rest
    of_ref = outs[0] if want_f32 else None
    ob_ref = outs[-1] if want_b16 else None
    acc = jnp.dot(x_ref[...], w_ref[...], preferred_element_type=F32)
    if kinds is None:
        if sigmoid:
            acc = jax.nn.sigmoid(acc)
        if want_f32:
            of_ref[...] = acc
        if want_b16:
            ob_ref[...] = acc.astype(BF16)
        return
    for c, (kind, scale) in enumerate(zip(kinds, scales)):
        sl = slice(c * LANE, (c + 1) * LANE)
        blk = acc[:, sl]
        if kind is not None:
            base = 0 if kind == "r128" else 3 * LANE
            half = (HD // 4) // 2 if kind == "r128" else (IDX_HD // 4) // 2
            cos = tab_ref[:, base:base + LANE]
            s_lo = tab_ref[:, base + LANE:base + 2 * LANE]
            s_hi = tab_ref[:, base + 2 * LANE:base + 3 * LANE]
            blk = (blk * cos + pltpu.roll(blk, LANE - half, 1) * s_lo
                   + pltpu.roll(blk, half, 1) * s_hi)
        if want_f32:
            of_ref[:, sl] = blk
        if want_b16:
            ob_ref[:, sl] = (blk * scale if scale != 1.0 else blk).astype(BF16)


def _matmul(x, w, *, tab=None, kinds=None, scales=None, sigmoid=False,
            want_f32=True, want_b16=False, tm_target=256, tn_target=2048):
    m, k = x.shape
    n = w.shape[1]
    tm = _tile(m, tm_target)
    tn = n if kinds is not None else _tile(n, tn_target, LANE)
    grid = (n // tn, m // tm)
    in_specs = [pl.BlockSpec((tm, k), lambda j, i: (i, 0)),
                pl.BlockSpec((k, tn), lambda j, i: (0, j))]
    args = [x, w]
    if tab is not None:
        in_specs.append(pl.BlockSpec((tm, tab.shape[1]), lambda j, i: (i, 0)))
        args.append(tab)
    out_shape, out_specs = [], []
    if want_f32:
        out_shape.append(jax.ShapeDtypeStruct((m, n), F32))
        out_specs.append(pl.BlockSpec((tm, tn), lambda j, i: (i, j)))
    if want_b16:
        out_shape.append(jax.ShapeDtypeStruct((m, n), BF16))
        out_specs.append(pl.BlockSpec((tm, tn), lambda j, i: (i, j)))
    body = functools.partial(_mm_body, kinds=kinds, scales=scales, has_tab=tab is not None,
                             sigmoid=sigmoid, want_f32=want_f32, want_b16=want_b16)
    return pl.pallas_call(body, grid=grid, in_specs=in_specs, out_specs=out_specs,
                          out_shape=out_shape, compiler_params=_params(2))(*args)


def _rope_tables(pos):
    tabs = []
    for hd in (HD, IDX_HD):
        rot = hd // 4
        half = rot // 2
        freqs = ROPE_THETA ** (-2.0 * jnp.arange(half, dtype=F32) / rot)
        ang = pos.astype(F32)[:, None] * freqs[None, :]
        cos, sin = jnp.cos(ang), jnp.sin(ang)
        m = pos.shape[0]
        one = jnp.ones((m, hd - rot), F32)
        zero_h = jnp.zeros((m, half), F32)
        zero_t = jnp.zeros((m, hd - rot), F32)
        c = jnp.concatenate([cos, cos, one], axis=1)
        s_lo = jnp.concatenate([-sin, zero_h, zero_t], axis=1)
        s_hi = jnp.concatenate([zero_h, sin, zero_t], axis=1)
        rep = LANE // hd
        tabs += [jnp.tile(c, (1, rep)), jnp.tile(s_lo, (1, rep)), jnp.tile(s_hi, (1, rep))]
    return jnp.concatenate(tabs, axis=1)


def _conv_body(u_ref, st_ref, w_ref, y_ref, ns_ref, carry, *, tm):
    @pl.when(pl.program_id(1) == 0)
    def _():
        carry[...] = st_ref[0]

    gate_b = u_ref[0, :, 0:BR_W]
    v = u_ref[0, :, BR_W:2 * BR_W] * u_ref[0, :, 2 * BR_W:3 * BR_W]
    p0 = carry[0:1, :]
    p1 = carry[1:2, :]
    row = lax.broadcasted_iota(jnp.int32, (tm, BR_W), 0)
    v1 = jnp.where(row == 0, p1, pltpu.roll(v, 1, 0))
    v2 = jnp.where(row == 0, p0, jnp.where(row == 1, p1, pltpu.roll(v, 2, 0)))
    w = w_ref[...]
    y = gate_b * (v2 * w[0:1, :] + v1 * w[1:2, :] + v * w[2:3, :])
    y_ref[0] = y.astype(BF16)
    tail = (u_ref[0, tm - 2:tm, BR_W:2 * BR_W] * u_ref[0, tm - 2:tm, 2 * BR_W:3 * BR_W])
    carry[...] = tail
    ns_ref[0] = tail


def _short_conv(u_a, state, conv_w):
    b, t, _ = u_a.shape
    assert t >= CONV_W - 1
    tm = _tile(t, 512)
    body = functools.partial(_conv_body, tm=tm)
    return pl.pallas_call(
        body, grid=(b, t // tm),
        in_specs=[pl.BlockSpec((1, tm, 3 * BR_W), lambda bi, i: (bi, i, 0)),
                  pl.BlockSpec((1, CONV_W - 1, BR_W), lambda bi, i: (bi, 0, 0)),
                  pl.BlockSpec((CONV_W, BR_W), lambda bi, i: (0, 0))],
        out_specs=[pl.BlockSpec((1, tm, BR_W), lambda bi, i: (bi, i, 0)),
                   pl.BlockSpec((1, CONV_W - 1, BR_W), lambda bi, i: (bi, 0, 0))],
        out_shape=[jax.ShapeDtypeStruct((b, t, BR_W), BF16),
                   jax.ShapeDtypeStruct((b, CONV_W - 1, BR_W), F32)],
        scratch_shapes=[pltpu.VMEM((CONV_W - 1, BR_W), F32)],
        compiler_params=_params(2))(u_a, state, conv_w)


def _fox_prep_body(x_ref, bias_ref, lf_ref, cum_ref, carry, *, tl, past_len):
    i = pl.program_id(1)

    @pl.when(i == 0)
    def _():
        carry[...] = jnp.zeros_like(carry)

    x = x_ref[0]
    z = x + bias_ref[...]
    log_sig = jnp.minimum(z, 0.0) - jnp.log1p(jnp.exp(-jnp.abs(z)))
    row = i * tl + lax.broadcasted_iota(jnp.int32, (tl, 1), 0)
    lf = jnp.where(row >= past_len, log_sig, x)
    r = lax.broadcasted_iota(jnp.int32, (tl, tl), 0)
    c = lax.broadcasted_iota(jnp.int32, (tl, tl), 1)
    tri = jnp.where(c <= r, 1.0, 0.0).astype(F32)
    cs = jnp.dot(tri, lf, precision=lax.Precision.HIGHEST, preferred_element_type=F32) + carry[...]
    lf_ref[0] = lf
    cum_ref[0] = cs
    carry[...] = cs[tl - 1:tl, :]


def _fox_prep(x, bias_row, past_len):
    b, l, _ = x.shape
    tl = _tile(l, 512)
    body = functools.partial(_fox_prep_body, tl=tl, past_len=past_len)
    spec = pl.BlockSpec((1, tl, LANE), lambda bi, i: (bi, i, 0))
    return pl.pallas_call(
        body, grid=(b, l // tl),
        in_specs=[spec, pl.BlockSpec((1, LANE), lambda bi, i: (0, 0))],
        out_specs=[spec, spec],
        out_shape=[jax.ShapeDtypeStruct((b, l, LANE), F32)] * 2,
        scratch_shapes=[pltpu.VMEM((1, LANE), F32)],
        compiler_params=_params(2))(x, bias_row)


def _row_limits(q0, tq, chunk, n_keys):
    qpos = q0 + lax.broadcasted_iota(jnp.int32, (tq, 1), 0)
    if chunk == 1:
        lim = qpos + 1
    else:
        sh = chunk.bit_length() - 1
        lim = lax.shift_left(lax.shift_right_arithmetic(qpos, sh) + 1, sh)
    return jnp.minimum(lim, n_keys)


def _tile_counts(q0, tq, tk, chunk, n_keys):
    lim_min = (q0 // chunk + 1) * chunk
    lim_max = ((q0 + tq - 1) // chunk + 1) * chunk
    n_full = jnp.minimum(lim_min, n_keys) // tk
    n_need = (jnp.minimum(lim_max, n_keys) + tk - 1) // tk
    return n_full, n_need


def _softmax_step(s, v, m_sc, l_sc, acc_sc):
    m_old = m_sc[...]
    m_new = jnp.maximum(m_old, jnp.max(s, axis=1, keepdims=True))
    alpha = jnp.exp(m_old - m_new)
    p = jnp.exp(s - m_new)
    l_sc[...] = alpha * l_sc[...] + jnp.sum(p, axis=1, keepdims=True)
    acc_sc[...] = alpha * acc_sc[...] + jnp.dot(p.astype(BF16), v, preferred_element_type=F32)
    m_sc[...] = m_new


def _qk(q, k):
    return lax.dot_general(q, k, (((1,), (1,)), ((), ())), preferred_element_type=F32)


def _flash_body(*refs, tq, tk, past_len, n_keys, chunk, mode):
    if mode == "fox":
        q_ref, k_ref, v_ref, ck_ref, o_ref, m_sc, l_sc, acc_sc = refs
    else:
        q_ref, k_ref, v_ref, lam_ref, gain_ref, li_ref, o_ref, m_sc, l_sc, acc_sc = refs
    q0 = past_len + pl.program_id(2) * tq
    n_full, n_need = _tile_counts(q0, tq, tk, chunk, n_keys)
    qlim = _row_limits(q0, tq, chunk, n_keys)
    if mode == "fox":
        q = q_ref[0]
    else:
        qf = q_ref[0].astype(F32)
        lane = lax.broadcasted_iota(jnp.int32, (tq, LANE), 1)
        q = jnp.concatenate([jnp.where(lane < D_QK, qf, 0.0), jnp.where(lane >= D_QK, qf, 0.0)],
                            axis=0).astype(BF16)
        qlim = jnp.concatenate([qlim, qlim], axis=0)
    m_sc[...] = jnp.full(m_sc.shape, NEG, F32)
    l_sc[...] = jnp.zeros(l_sc.shape, F32)
    acc_sc[...] = jnp.zeros(acc_sc.shape, F32)

    def step(j, masked):
        ks = pl.multiple_of(j * tk, tk)
        s = _qk(q, k_ref[0, pl.ds(ks, tk), :])
        if mode == "fox":
            s = s - ck_ref[0, 0, j]
        if masked:
            kpos = ks + lax.broadcasted_iota(jnp.int32, (1, tk), 1)
            s = jnp.where(kpos < qlim, s, NEG)
        _softmax_step(s, v_ref[0, pl.ds(ks, tk), :], m_sc, l_sc, acc_sc)

    def full_step(j, c):
        step(j, False)
        return c

    def masked_step(j, c):
        step(j, True)
        return c

    lax.fori_loop(0, n_full, full_step, 0)
    lax.fori_loop(n_full, n_need, masked_step, 0)

    o = acc_sc[...] / l_sc[...]
    if mode == "diff":
        lam = lam_ref[...]
        li = li_ref[...]
        lmb = (jnp.exp(jnp.sum(lam[0:1, :] * lam[1:2, :], axis=1, keepdims=True))
               - jnp.exp(jnp.sum(lam[2:3, :] * lam[3:4, :], axis=1, keepdims=True)) + li)
        o = o[0:tq, :] - lmb * o[tq:2 * tq, :]
        o = o * lax.rsqrt(jnp.mean(o * o, axis=1, keepdims=True) + RMS_EPS)
        o = o * gain_ref[...] * (1.0 - li)
    o_ref[0] = o.astype(BF16)


def _flash(mode, q, q_off, k, k_off, v, v_off, extra, *, past_len, n_keys, tq, tk):
    b, t, _ = q.shape
    lpad = k.shape[1]
    heads = C_HEADS if mode == "fox" else D_HEADS
    stack = 1 if mode == "fox" else 2
    chunk = 1 if mode == "fox" else CHUNK
    in_specs = [pl.BlockSpec((1, tq, LANE), lambda bi, h, i: (bi, i, q_off + h)),
                pl.BlockSpec((1, lpad, LANE), lambda bi, h, i: (bi, 0, k_off + h)),
                pl.BlockSpec((1, lpad, LANE), lambda bi, h, i: (bi, 0, v_off + h))]
    if mode == "fox":
        (ck,) = extra
        in_specs.append(pl.BlockSpec((1, 1, lpad // tk, 1, tk), lambda bi, h, i: (bi, h, 0, 0, 0)))
    else:
        lam, gain, li = extra
        in_specs += [pl.BlockSpec(lam.shape, lambda bi, h, i: (0, 0)),
                     pl.BlockSpec(gain.shape, lambda bi, h, i: (0, 0)),
                     pl.BlockSpec(li.shape, lambda bi, h, i: (0, 0))]
    body = functools.partial(_flash_body, tq=tq, tk=tk, past_len=past_len, n_keys=n_keys,
                             chunk=chunk, mode=mode)
    rows = stack * tq
    return pl.pallas_call(
        body, grid=(b, heads, t // tq), in_specs=in_specs,
        out_specs=pl.BlockSpec((1, tq, LANE), lambda bi, h, i: (bi, i, h)),
        out_shape=jax.ShapeDtypeStruct((b, t, heads * LANE), BF16),
        scratch_shapes=[pltpu.VMEM((rows, 1), F32), pltpu.VMEM((rows, 1), F32),
                        pltpu.VMEM((rows, LANE), F32)],
        compiler_params=_params(3))(q, k, v, *extra)


def _dsa_body(q_ref, iq_ref, iw_ref, k_ref, v_ref, ik_ref, o_ref, keys, m_sc, l_sc, acc_sc,
              *, tq, tk, past_len, n_keys, k_sel, idx_bits):
    q0 = past_len + pl.program_id(1) * tq
    _, n_need = _tile_counts(q0, tq, tk, CHUNK, n_keys)
    qlim = _row_limits(q0, tq, CHUNK, n_keys)

    iqf = iq_ref[0]
    parts = []
    for h in range(IDX_HEADS):
        blk = iqf[:, (h // 2) * LANE:(h // 2 + 1) * LANE]
        parts.append(pltpu.roll(blk, IDX_HD, 1) if h % 2 else blk)
    iq = jnp.concatenate(parts, axis=0).astype(BF16)
    iw = iw_ref[0]
    w_cols = [iw[:, IW_LANE + h:IW_LANE + h + 1] for h in range(IDX_HEADS)]

    def score_tile(j, c):
        ks = pl.multiple_of(j * tk, tk)
        rel = _qk(iq, ik_ref[0, pl.ds(ks, tk), :])
        sc = jnp.zeros((tq, tk), F32)
        for h in range(IDX_HEADS):
            sc = sc + jnp.maximum(rel[h * tq:(h + 1) * tq, :], 0.0) * w_cols[h]
        sc = sc * IDX_SCALE
        sc = jnp.where(sc == 0.0, 0.0, sc)
        bits = lax.bitcast_convert_type(sc, jnp.int32)
        key = jnp.where(bits < 0, bits ^ jnp.int32(0x7FFFFFFF), bits)
        kpos = ks + lax.broadcasted_iota(jnp.int32, (1, tk), 1)
        keys[j] = jnp.where(kpos < qlim, key, INT_MIN)
        return c

    lax.fori_loop(0, n_need, score_tile, 0)

    def count(pred):
        def body(j, acc):
            hit = jnp.where(pred(keys[j], j * tk), 1.0, 0.0)
            for c in range(tk // LANE):
                acc = acc + hit[:, c * LANE:(c + 1) * LANE]
            return acc
        acc = lax.fori_loop(0, n_need, body, jnp.zeros((tq, LANE), F32))
        return jnp.sum(acc, axis=1, keepdims=True)

    def count_ge(cand):
        return count(lambda t, _: t >= cand)

    want = jnp.float32(k_sel)
    kstar = jnp.where(count_ge(jnp.zeros((tq, 1), jnp.int32)) >= want, jnp.int32(0), INT_MIN)

    def value_bit(b, ans):
        cand = ans + lax.shift_left(jnp.int32(1), 30 - b)
        return jnp.where(count_ge(cand) >= want, cand, ans)

    kstar = lax.fori_loop(0, 31, value_bit, kstar)

    n_ge = count_ge(kstar)
    need = want - count_ge(kstar + 1)
    tie = (n_ge > want) & (kstar > INT_MIN)

    @pl.when(jnp.max(jnp.where(tie, 1.0, 0.0)) > 0.0)
    def _():
        def pos_of(base):
            return base + lax.broadcasted_iota(jnp.int32, (1, tk), 1)

        def index_bit(b, ans):
            cand = ans + lax.shift_left(jnp.int32(1), idx_bits - 1 - b)
            n_before = count(lambda t, base: (t == kstar) & (pos_of(base) < cand))
            return jnp.where(n_before < need, cand, ans)

        last = lax.fori_loop(0, idx_bits, index_bit, jnp.zeros((tq, 1), jnp.int32))

        def drop(j, c):
            t = keys[j]
            keys[j] = jnp.where(tie & (t == kstar) & (pos_of(j * tk) > last), INT_MIN, t)
            return c

        lax.fori_loop(0, n_need, drop, 0)

    thr = jnp.maximum(kstar, INT_MIN + 1)
    qf = q_ref[0]
    q = jnp.concatenate([qf[:, h * LANE:(h + 1) * LANE] for h in range(B_HEADS)], axis=0)
    m_sc[...] = jnp.full(m_sc.shape, NEG, F32)
    l_sc[...] = jnp.zeros(l_sc.shape, F32)
    acc_sc[...] = jnp.zeros(acc_sc.shape, F32)

    def attend(j, c):
        ks = pl.multiple_of(j * tk, tk)
        s = _qk(q, k_ref[0, pl.ds(ks, tk), :])
        bias = jnp.where(keys[j] >= thr, 0.0, NEG)
        s = s + jnp.concatenate([bias] * B_HEADS, axis=0)
        _softmax_step(s, v_ref[0, pl.ds(ks, tk), :], m_sc, l_sc, acc_sc)
        return c

    lax.fori_loop(0, n_need, attend, 0)
    o = acc_sc[...] / l_sc[...]
    for h in range(B_HEADS):
        o_ref[0, :, h * LANE:(h + 1) * LANE] = o[h * tq:(h + 1) * tq, :].astype(BF16)


def _dsa(q, iq, iw, iw_off, k, k_off, v, v_off, ik, ik_off, *, past_len, n_keys, tq, tk):
    b, t, _ = q.shape
    lpad = k.shape[1]
    k_sel = min(TOPK_MAX, n_keys // 4)
    body = functools.partial(_dsa_body, tq=tq, tk=tk, past_len=past_len, n_keys=n_keys,
                             k_sel=k_sel, idx_bits=max(1, (lpad - 1).bit_length()))
    rows = B_HEADS * tq
    return pl.pallas_call(
        body, grid=(b, t // tq),
        in_specs=[pl.BlockSpec((1, tq, B_HEADS * LANE), lambda bi, i: (bi, i, R_BQ // B_HEADS)),
                  pl.BlockSpec((1, tq, 2 * LANE), lambda bi, i: (bi, i, R_BIQ // 2)),
                  pl.BlockSpec((1, tq, LANE), lambda bi, i: (bi, i, iw_off)),
                  pl.BlockSpec((1, lpad, LANE), lambda bi, i: (bi, 0, k_off)),
                  pl.BlockSpec((1, lpad, LANE), lambda bi, i: (bi, 0, v_off)),
                  pl.BlockSpec((1, lpad, LANE), lambda bi, i: (bi, 0, ik_off))],
        out_specs=pl.BlockSpec((1, tq, B_HEADS * LANE), lambda bi, i: (bi, i, 0)),
        out_shape=jax.ShapeDtypeStruct((b, t, B_HEADS * LANE), BF16),
        scratch_shapes=[pltpu.VMEM((lpad // tk, tq, tk), jnp.int32),
                        pltpu.VMEM((rows, 1), F32), pltpu.VMEM((rows, 1), F32),
                        pltpu.VMEM((rows, LANE), F32)],
        compiler_params=_params(2))(q, iq, iw, k, v, ik)


def _merge_body(ya_ref, yb_ref, yc_ref, yd_ref, g_ref, wb_ref, o_ref):
    d = o_ref.shape[1]
    acc = 0.0
    for m, y_ref in enumerate((ya_ref, yb_ref, yc_ref, yd_ref)):
        acc = acc + g_ref[:, m * d:(m + 1) * d].astype(F32) * jnp.dot(
            y_ref[...], wb_ref[m], preferred_element_type=F32)
    o_ref[...] = acc.astype(BF16)


def _merge(ys, gate, w_branch):
    m, d = gate.shape[0], w_branch.shape[2]
    tm = _tile(m, 256)
    y_spec = pl.BlockSpec((tm, BR_W), lambda i: (i, 0))
    return pl.pallas_call(
        _merge_body, grid=(m // tm,),
        in_specs=[y_spec] * N_BRANCH + [pl.BlockSpec((tm, N_BRANCH * d), lambda i: (i, 0)),
                                        pl.BlockSpec(w_branch.shape, lambda i: (0, 0, 0))],
        out_specs=pl.BlockSpec((tm, d), lambda i: (i, 0)),
        out_shape=jax.ShapeDtypeStruct((m, d), BF16),
        compiler_params=_params(1))(*ys, gate, w_branch)


def _layernorm(z, g, b):
    mu = jnp.mean(z, axis=-1, keepdims=True)
    zc = z - mu
    var = jnp.mean(zc * zc, axis=-1, keepdims=True)
    return zc * lax.rsqrt(var + LN_EPS) * g + b


def _wo_ln_body(m_ref, x_ref, w_ref, g_ref, b_ref, of_ref, ob_ref, *, alpha):
    h = jnp.dot(m_ref[...], w_ref[...], preferred_element_type=F32)
    y = _layernorm(alpha * x_ref[...] + h, g_ref[...], b_ref[...])
    of_ref[...] = y
    ob_ref[...] = y.astype(BF16)


def _wo_ln(merged, x, w_o, g, b, alpha):
    m, d = x.shape
    tm = _tile(m, 256)
    row = pl.BlockSpec((tm, d), lambda i: (i, 0))
    vec = pl.BlockSpec((1, d), lambda i: (0, 0))
    return pl.pallas_call(
        functools.partial(_wo_ln_body, alpha=alpha), grid=(m // tm,),
        in_specs=[row, row, pl.BlockSpec((d, d), lambda i: (0, 0)), vec, vec],
        out_specs=[row, row],
        out_shape=[jax.ShapeDtypeStruct((m, d), F32), jax.ShapeDtypeStruct((m, d), BF16)],
        compiler_params=_params(1))(merged, x, w_o, g, b)


def _ffn_body(xb_ref, xf_ref, wu_ref, wd_ref, g_ref, b_ref, of_ref, ob_ref, acc, *, alpha):
    f = pl.program_id(1)

    @pl.when(f == 0)
    def _():
        acc[...] = jnp.zeros_like(acc)

    h = jnp.maximum(jnp.dot(xb_ref[...], wu_ref[...], preferred_element_type=F32), 0.0)
    acc[...] += jnp.dot((h * h).astype(BF16), wd_ref[...], preferred_element_type=F32)

    @pl.when(f == pl.num_programs(1) - 1)
    def _():
        y = _layernorm(alpha * xf_ref[...] + acc[...], g_ref[...], b_ref[...])
        of_ref[...] = y
        ob_ref[...] = y.astype(BF16)


def _ffn(xb, xf, w_up, w_down, g, b, alpha):
    m, d = xf.shape
    dff = w_up.shape[1]
    tm = _tile(m, 512)
    tf = _tile(dff, 512, LANE)
    row = pl.BlockSpec((tm, d), lambda i, f: (i, 0))
    vec = pl.BlockSpec((1, d), lambda i, f: (0, 0))
    return pl.pallas_call(
        functools.partial(_ffn_body, alpha=alpha), grid=(m // tm, dff // tf),
        in_specs=[row, row, pl.BlockSpec((d, tf), lambda i, f: (0, f)),
                  pl.BlockSpec((tf, d), lambda i, f: (f, 0)), vec, vec],
        out_specs=[row, row],
        out_shape=[jax.ShapeDtypeStruct((m, d), F32), jax.ShapeDtypeStruct((m, d), BF16)],
        scratch_shapes=[pltpu.VMEM((tm, d), F32)],
        compiler_params=_params(2))(xb, xf, w_up, w_down, g, b)


def _prep_weights(w_in, w_branch, w_o, w_up, w_down):
    d = w_in.shape[1]
    sizes = (BR_W, BR_W, BR_W, B_HEADS * HD, HD, HD, IDX_HEADS * IDX_HD, IDX_HD, IDX_HEADS,
             C_HEADS * HD, C_HEADS * HD, C_HEADS * HD, C_HEADS,
             D_HEADS * 2 * D_QK, D_HEADS * 2 * D_QK, D_HEADS * D_V, N_BRANCH * d)
    names = ("a_b", "a_c", "a_h", "b_q", "b_k", "b_v", "b_iq", "b_ik", "b_iw",
             "c_q", "c_k", "c_v", "c_f", "d_q", "d_k", "d_v", "gate")
    starts = np.concatenate([[0], np.cumsum(sizes)])
    col = {n: w_in[:, :, int(starts[i]):int(starts[i + 1])] for i, n in enumerate(names)}
    depth = w_in.shape[0]

    def zeros(n):
        return jnp.zeros((depth, d, n), w_in.dtype)

    w_a = jnp.concatenate([col["a_b"], col["a_c"], col["a_h"]], axis=2)
    w_r = jnp.concatenate([col["b_q"], col["b_iq"], col["b_k"], col["b_ik"], zeros(LANE - IDX_HD),
                           col["d_q"], col["d_k"]], axis=2)
    w_p = jnp.concatenate([col["c_q"], col["c_k"], col["c_v"], col["d_v"], col["b_v"],
                           col["b_iw"], col["c_f"], zeros(LANE - IDX_HEADS - C_HEADS)], axis=2)
    assert w_r.shape[2] == R_BLOCKS * LANE and w_p.shape[2] == P_BLOCKS * LANE
    cast = lambda w: w.astype(BF16)
    return dict(w_a=cast(w_a), w_r=cast(w_r), w_p=cast(w_p), w_g=cast(col["gate"]),
                w_branch=cast(w_branch), w_o=cast(w_o), w_up=cast(w_up), w_down=cast(w_down))


def _cols(a, block, n_blocks=1):
    return a[:, :, block * LANE:(block + n_blocks) * LANE]


def _trunk(x, past_len, conv_state, dsa_rows, fox_rows, diff_rows, wts, conv_w, fox_fbias,
           diff_lambda, diff_gain, ln1_g, ln1_b, ln2_g, ln2_b):
    b, t, d = x.shape
    depth = conv_w.shape[0]
    alpha = (2 * depth) ** 0.25
    m = b * t
    n_keys = past_len + t
    lpad = -(-n_keys // LANE) * LANE
    pad = lpad - n_keys
    pos = jnp.tile(past_len + jnp.arange(t, dtype=jnp.int32), b)
    tab = _rope_tables(pos)
    if past_len:
        tq_att = tq_dsa = t
        tk_att = tk_dsa = lpad
    else:
        tq_att = _tile(t, 512)
        tk_att = _tile(lpad, 512, LANE)
        tq_dsa = _tile(t, 128)
        tk_dsa = tk_att

    def key_rows(past, new):
        rows = jnp.concatenate([past, new], axis=1)
        return jnp.pad(rows, ((0, 0), (0, pad), (0, 0))).astype(BF16)

    xf = x.reshape(m, d)
    xb = xf.astype(BF16)
    conv_out, dsa_out, fox_out, diff_out = [], [], [], []
    for l in range(depth):
        lam_init = 0.8 - 0.6 * float(np.exp(-0.3 * l))
        u_a = _matmul(xb, wts["w_a"][l])[0].reshape(b, t, 3 * BR_W)
        rf, rb = _matmul(xb, wts["w_r"][l], tab=tab, kinds=R_KINDS, scales=R_SCALES, want_b16=True)
        pf, pb = _matmul(xb, wts["w_p"][l], kinds=P_KINDS, scales=P_SCALES, want_b16=True)
        (gate,) = _matmul(xb, wts["w_g"][l], sigmoid=True, want_f32=False, want_b16=True,
                          tm_target=512)
        rf, rb, pf, pb = (a.reshape(b, t, a.shape[1]) for a in (rf, rb, pf, pb))

        y_a, n_conv = _short_conv(u_a, conv_state[l], conv_w[l])

        small = _cols(pf, P_SMALL)
        bias_row = jnp.zeros((1, LANE), F32).at[0, CF_LANE:CF_LANE + C_HEADS].set(fox_fbias[l])
        if past_len:
            past_lf = jnp.pad(fox_rows[l][..., 2 * HD],
                              ((0, 0), (0, 0), (CF_LANE, LANE - CF_LANE - C_HEADS)))
            small_all = jnp.concatenate([past_lf, small], axis=1)
        else:
            small_all = small
        lf, cum = _fox_prep(small_all, bias_row, past_len)
        lf_new = lf[:, past_len:, CF_LANE:CF_LANE + C_HEADS]
        ck = jnp.transpose(cum[:, :, CF_LANE:CF_LANE + C_HEADS], (0, 2, 1))
        ck = jnp.pad(ck, ((0, 0), (0, 0), (0, pad))).reshape(b, C_HEADS, lpad // tk_att, 1, tk_att)
        if past_len:
            lp = past_len
            fk = key_rows(fox_rows[l][..., 0:HD].reshape(b, lp, C_HEADS * HD), _cols(pf, P_CK, 4))
            fv = key_rows(fox_rows[l][..., HD:2 * HD].reshape(b, lp, C_HEADS * HD), _cols(pf, P_CV, 4))
            dk = key_rows(diff_rows[l][..., 0:2 * D_QK].reshape(b, lp, D_HEADS * 2 * D_QK),
                          _cols(rf, R_DK, 4))
            dv = key_rows(diff_rows[l][..., 2 * D_QK:].reshape(b, lp, D_HEADS * D_V), _cols(pf, P_DV, 4))
            bk = key_rows(dsa_rows[l][..., 0:HD], _cols(rf, R_BK))
            bv = key_rows(dsa_rows[l][..., HD:2 * HD], _cols(pf, P_BV))
            bik = key_rows(jnp.pad(dsa_rows[l][..., 2 * HD:], ((0, 0), (0, 0), (0, LANE - IDX_HD))),
                           _cols(rf, R_BIK))
            fox_kv = (fk, 0, fv, 0)
            diff_kv = (dk, 0, dv, 0)
            dsa_kv = (bk, 0, bv, 0, bik, 0)
        else:
            fox_kv = (pb, P_CK, pb, P_CV)
            diff_kv = (rb, R_DK, pb, P_DV)
            dsa_kv = (rb, R_BK, pb, P_BV, rb, R_BIK)
        att = dict(past_len=past_len, n_keys=n_keys)
        y_c = _flash("fox", pb, P_CQ, *fox_kv, (ck,), tq=tq_att, tk=tk_att, **att)
        li = jnp.full((1, 1), lam_init, F32)
        y_d = _flash("diff", rb, R_DQ, *diff_kv, (diff_lambda[l], diff_gain[l].reshape(1, D_V), li),
                     tq=tq_att, tk=tk_att, **att)
        y_b = _dsa(rb, rf, pf, P_SMALL, *dsa_kv, tq=tq_dsa, tk=tk_dsa, **att)

        ys = [y.reshape(m, BR_W) for y in (y_a, y_b, y_c, y_d)]
        merged = _merge(ys, gate, wts["w_branch"][l])
        xf, xb = _wo_ln(merged, xf, wts["w_o"][l], ln1_g[l].reshape(1, d), ln1_b[l].reshape(1, d), alpha)
        xf, xb = _ffn(xb, xf, wts["w_up"][l], wts["w_down"][l], ln2_g[l].reshape(1, d),
                      ln2_b[l].reshape(1, d), alpha)

        conv_out.append(n_conv)
        dsa_out.append(jnp.concatenate([_cols(rf, R_BK), _cols(pf, P_BV), _cols(rf, R_BIK)[..., :IDX_HD]],
                                       axis=-1))
        fox_out.append(jnp.concatenate([_cols(pf, P_CK, 4).reshape(b, t, C_HEADS, HD),
                                        _cols(pf, P_CV, 4).reshape(b, t, C_HEADS, HD),
                                        lf_new[..., None]], axis=-1))
        diff_out.append(jnp.concatenate([_cols(rf, R_DK, 4).reshape(b, t, D_HEADS, 2 * D_QK),
                                         _cols(pf, P_DV, 4).reshape(b, t, D_HEADS, D_V)], axis=-1))
    return (xf.reshape(b, t, d), jnp.stack(conv_out), jnp.stack(dsa_out), jnp.stack(fox_out),
            jnp.stack(diff_out))


def kernel(x_prompt, x_sample, state_conv, cache_dsa, cache_fox, cache_diff, w_in, conv_w, fox_fbias,
           diff_lambda, diff_gain, w_branch, w_o, ln1_g, ln1_b, w_up, w_down, ln2_g, ln2_b):
    depth = w_in.shape[0]
    b = x_prompt.shape[0]
    wts = _prep_weights(w_in, w_branch, w_o, w_up, w_down)
    shared = (wts, conv_w, fox_fbias, diff_lambda, diff_gain, ln1_g, ln1_b, ln2_g, ln2_b)
    conv0 = jnp.zeros((depth, b, CONV_W - 1, BR_W), x_prompt.dtype)
    y_p, conv_p, dsa_p, fox_p, diff_p = _trunk(x_prompt, 0, conv0, None, None, None, *shared)
    y_s, conv_s, dsa_s, fox_s, diff_s = _trunk(x_sample, cache_dsa.shape[2], state_conv, cache_dsa,
                                               cache_fox, cache_diff, *shared)
    return (y_p, y_s, conv_p, dsa_p, fox_p, diff_p, conv_s, dsa_s, fox_s, diff_s)
```

```python
import functools

import numpy as np
import jax
import jax.numpy as jnp
from jax import lax
from jax.experimental import pallas as pl
from jax.experimental.pallas import tpu as pltpu

CHUNK = 64
N_BRANCH = 4
BR_W = 512
HD = 128
CONV_W = 3
B_HEADS = 4
IDX_HEADS = 4
IDX_HD = 64
TOPK_MAX = 256
IDX_SCALE = (IDX_HEADS * IDX_HD) ** -0.5
C_HEADS = 4
D_HEADS = 4
D_QK = 64
D_V = 128
ROPE_THETA = 500000.0
LN_EPS = 1e-5
RMS_EPS = 1e-5

LANE = 128
STRIP = 16
VMEM_LIMIT = 56 * 1024 * 1024
NEG = -1e30
LOG2E = 1.4426950408889634
INT_MIN = np.int32(-2 ** 31)
F32 = jnp.float32
BF16 = jnp.bfloat16

R_BQ, R_BIQ, R_BK, R_BIK, R_DQ, R_DK, R_BLOCKS = 0, 4, 6, 7, 8, 12, 16
P_CQ, P_CK, P_CV, P_DV, P_BV, P_SMALL, P_BLOCKS = 0, 4, 8, 12, 16, 17, 18
IW_LANE, CF_LANE = 0, IDX_HEADS
R_KINDS = ("r128",) * 4 + ("r64",) * 2 + ("r128", "r64") + ("r64",) * 8
R_SCALES = (HD ** -0.5 * LOG2E,) * 4 + (1.0,) * 4 + (D_QK ** -0.5 * LOG2E,) * 4 + (1.0,) * 4
P_KINDS = (None,) * P_BLOCKS
P_SCALES = (HD ** -0.5 * LOG2E,) * 4 + (1.0,) * 14


def _tile(n, target, align=8):
    if n <= target:
        return n
    for t in range(target - target % align, 0, -align):
        if n % t == 0:
            return t
    return n


def _params(n_grid):
    return pltpu.CompilerParams(dimension_semantics=("arbitrary",) * n_grid,
                                vmem_limit_bytes=VMEM_LIMIT)


def _mm_body(x_ref, w_ref, *rest, kinds, scales, has_tab, sigmoid, want_f32, want_b16):
    tab_ref = rest[0] if has_tab else None
    outs = rest[1:] if has_tab else rest
    of_ref = outs[0] if want_f32 else None
    ob_ref = outs[-1] if want_b16 else None
    acc = jnp.dot(x_ref[...], w_ref[...], preferred_element_type=F32)
    if kinds is None:
        if sigmoid:
            acc = jax.nn.sigmoid(acc)
        if want_f32:
            of_ref[...] = acc
        if want_b16:
            ob_ref[...] = acc.astype(BF16)
        return
    for c, (kind, scale) in enumerate(zip(kinds, scales)):
        sl = slice(c * LANE, (c + 1) * LANE)
        blk = acc[:, sl]
        if kind is not None:
            base = 0 if kind == "r128" else 3 * LANE
            half = (HD // 4) // 2 if kind == "r128" else (IDX_HD // 4) // 2
            cos = tab_ref[:, base:base + LANE]
            s_lo = tab_ref[:, base + LANE:base + 2 * LANE]
            s_hi = tab_ref[:, base + 2 * LANE:base + 3 * LANE]
            blk = (blk * cos + pltpu.roll(blk, LANE - half, 1) * s_lo
                   + pltpu.roll(blk, half, 1) * s_hi)
        if want_f32:
            of_ref[:, sl] = blk
        if want_b16:
            ob_ref[:, sl] = (blk * scale if scale != 1.0 else blk).astype(BF16)


def _matmul(name, x, w, *, tab=None, kinds=None, scales=None, sigmoid=False,
            want_f32=True, want_b16=False, tm_target=256, tn_target=2048):
    m, k = x.shape
    n = w.shape[1]
    tm = _tile(m, tm_target)
    tn = n if kinds is not None else _tile(n, tn_target, LANE)
    grid = (n // tn, m // tm)
    in_specs = [pl.BlockSpec((tm, k), lambda j, i: (i, 0)),
                pl.BlockSpec((k, tn), lambda j, i: (0, j))]
    args = [x, w]
    if tab is not None:
        in_specs.append(pl.BlockSpec((tm, tab.shape[1]), lambda j, i: (i, 0)))
        args.append(tab)
    out_shape, out_specs = [], []
    if want_f32:
        out_shape.append(jax.ShapeDtypeStruct((m, n), F32))
        out_specs.append(pl.BlockSpec((tm, tn), lambda j, i: (i, j)))
    if want_b16:
        out_shape.append(jax.ShapeDtypeStruct((m, n), BF16))
        out_specs.append(pl.BlockSpec((tm, tn), lambda j, i: (i, j)))
    body = functools.partial(_mm_body, kinds=kinds, scales=scales, has_tab=tab is not None,
                             sigmoid=sigmoid, want_f32=want_f32, want_b16=want_b16)
    return pl.pallas_call(body, grid=grid, in_specs=in_specs, out_specs=out_specs,
                          out_shape=out_shape, compiler_params=_params(2), name=name)(*args)


def _rope_tables(pos):
    tabs = []
    for hd in (HD, IDX_HD):
        rot = hd // 4
        half = rot // 2
        freqs = ROPE_THETA ** (-2.0 * jnp.arange(half, dtype=F32) / rot)
        ang = pos.astype(F32)[:, None] * freqs[None, :]
        cos, sin = jnp.cos(ang), jnp.sin(ang)
        m = pos.shape[0]
        one = jnp.ones((m, hd - rot), F32)
        zero_h = jnp.zeros((m, half), F32)
        zero_t = jnp.zeros((m, hd - rot), F32)
        c = jnp.concatenate([cos, cos, one], axis=1)
        s_lo = jnp.concatenate([-sin, zero_h, zero_t], axis=1)
        s_hi = jnp.concatenate([zero_h, sin, zero_t], axis=1)
        rep = LANE // hd
        tabs += [jnp.tile(c, (1, rep)), jnp.tile(s_lo, (1, rep)), jnp.tile(s_hi, (1, rep))]
    return jnp.concatenate(tabs, axis=1)


def _conv_body(u_ref, st_ref, w_ref, y_ref, ns_ref, carry, *, tm):
    @pl.when(pl.program_id(1) == 0)
    def _():
        carry[...] = st_ref[0]

    gate_b = u_ref[0, :, 0:BR_W]
    v = u_ref[0, :, BR_W:2 * BR_W] * u_ref[0, :, 2 * BR_W:3 * BR_W]
    p0 = carry[0:1, :]
    p1 = carry[1:2, :]
    row = lax.broadcasted_iota(jnp.int32, (tm, BR_W), 0)
    v1 = jnp.where(row == 0, p1, pltpu.roll(v, 1, 0))
    v2 = jnp.where(row == 0, p0, jnp.where(row == 1, p1, pltpu.roll(v, 2, 0)))
    w = w_ref[...]
    y = gate_b * (v2 * w[0:1, :] + v1 * w[1:2, :] + v * w[2:3, :])
    y_ref[0] = y.astype(BF16)
    tail = (u_ref[0, tm - 2:tm, BR_W:2 * BR_W] * u_ref[0, tm - 2:tm, 2 * BR_W:3 * BR_W])
    carry[...] = tail
    ns_ref[0] = tail


def _short_conv(u_a, state, conv_w):
    b, t, _ = u_a.shape
    assert t >= CONV_W - 1
    tm = _tile(t, 512)
    body = functools.partial(_conv_body, tm=tm)
    return pl.pallas_call(
        body, grid=(b, t // tm),
        in_specs=[pl.BlockSpec((1, tm, 3 * BR_W), lambda bi, i: (bi, i, 0)),
                  pl.BlockSpec((1, CONV_W - 1, BR_W), lambda bi, i: (bi, 0, 0)),
                  pl.BlockSpec((CONV_W, BR_W), lambda bi, i: (0, 0))],
        out_specs=[pl.BlockSpec((1, tm, BR_W), lambda bi, i: (bi, i, 0)),
                   pl.BlockSpec((1, CONV_W - 1, BR_W), lambda bi, i: (bi, 0, 0))],
        out_shape=[jax.ShapeDtypeStruct((b, t, BR_W), BF16),
                   jax.ShapeDtypeStruct((b, CONV_W - 1, BR_W), F32)],
        scratch_shapes=[pltpu.VMEM((CONV_W - 1, BR_W), F32)],
        compiler_params=_params(2), name="short_conv")(u_a, state, conv_w)


def _fox_prep_body(x_ref, bias_ref, lf_ref, cum_ref, carry, *, tl, past_len):
    i = pl.program_id(1)

    @pl.when(i == 0)
    def _():
        carry[...] = jnp.zeros_like(carry)

    x = x_ref[0]
    z = x + bias_ref[...]
    log_sig = jnp.minimum(z, 0.0) - jnp.log1p(jnp.exp(-jnp.abs(z)))
    row = i * tl + lax.broadcasted_iota(jnp.int32, (tl, 1), 0)
    lf = jnp.where(row >= past_len, log_sig, x)
    r = lax.broadcasted_iota(jnp.int32, (tl, tl), 0)
    c = lax.broadcasted_iota(jnp.int32, (tl, tl), 1)
    tri = jnp.where(c <= r, 1.0, 0.0).astype(F32)
    cs = jnp.dot(tri, lf, precision=lax.Precision.HIGHEST, preferred_element_type=F32) + carry[...]
    lf_ref[0] = lf
    cum_ref[0] = cs
    carry[...] = cs[tl - 1:tl, :]


def _fox_prep(x, bias_row, past_len):
    b, l, _ = x.shape
    tl = _tile(l, 512)
    body = functools.partial(_fox_prep_body, tl=tl, past_len=past_len)
    spec = pl.BlockSpec((1, tl, LANE), lambda bi, i: (bi, i, 0))
    return pl.pallas_call(
        body, grid=(b, l // tl),
        in_specs=[spec, pl.BlockSpec((1, LANE), lambda bi, i: (0, 0))],
        out_specs=[spec, spec],
        out_shape=[jax.ShapeDtypeStruct((b, l, LANE), F32)] * 2,
        scratch_shapes=[pltpu.VMEM((1, LANE), F32)],
        compiler_params=_params(2), name="fox_prep")(x, bias_row)


def _row_limits(qpos, chunk):
    if chunk == 1:
        return qpos + 1
    sh = chunk.bit_length() - 1
    return lax.shift_left(lax.shift_right_arithmetic(qpos, sh) + 1, sh)


def _tile_counts(q0, tq, tk, chunk, n_keys):
    lim_min = (q0 // chunk + 1) * chunk
    lim_max = ((q0 + tq - 1) // chunk + 1) * chunk
    n_full = jnp.minimum(lim_min, n_keys) // tk
    n_need = (jnp.minimum(lim_max, n_keys) + tk - 1) // tk
    return n_full, n_need


def _softmax_scratch(rows, tk):
    return [pltpu.VMEM((rows, LANE), F32)] * 4 + [pltpu.VMEM((rows, tk), BF16)]


def _softmax_init(m_sc, l_sc, acc_sc):
    m_sc[...] = jnp.full(m_sc.shape, NEG, F32)
    l_sc[...] = jnp.zeros(l_sc.shape, F32)
    acc_sc[...] = jnp.zeros(acc_sc.shape, F32)


def _softmax_tile(s, pv, state, fix=None, base=0):
    m_sc, l_sc, a_sc, acc_sc, p_sc = state
    rows, tk = s.shape
    for r0 in range(0, rows, STRIP):
        rs = slice(base + r0, base + r0 + STRIP)
        x = s[r0:r0 + STRIP, :]
        if fix is not None:
            x = fix(x, r0)
        blocks = [x[:, c * LANE:(c + 1) * LANE] for c in range(tk // LANE)]
        mx = blocks[0]
        for blk in blocks[1:]:
            mx = jnp.maximum(mx, blk)
        m_old = m_sc[rs, :]
        m_new = jnp.maximum(m_old, jnp.max(mx, axis=1, keepdims=True))
        alpha = jnp.exp2(m_old - m_new)
        tot = None
        for c, blk in enumerate(blocks):
            p = jnp.exp2(blk - m_new)
            tot = p if tot is None else tot + p
            p_sc[rs, c * LANE:(c + 1) * LANE] = p.astype(BF16)
        l_sc[rs, :] = alpha * l_sc[rs, :] + tot
        m_sc[rs, :] = m_new
        a_sc[rs, :] = alpha
    rs = slice(base, base + rows)
    acc_sc[rs, :] = a_sc[rs, :] * acc_sc[rs, :] + pv(p_sc[rs, 0:tk])


def _pv(v):
    return lambda p: jnp.dot(p, v, preferred_element_type=F32)


def _softmax_out(state):
    _, l_sc, _, acc_sc, _ = state
    return acc_sc[...] / jnp.sum(l_sc[...], axis=1, keepdims=True)


def _qk(q, k):
    return lax.dot_general(q, k, (((1,), (1,)), ((), ())), preferred_element_type=F32)


def _split_subheads(q):
    qf = q.astype(F32)
    lane = lax.broadcasted_iota(jnp.int32, qf.shape, 1)
    return jnp.concatenate([jnp.where(lane < D_QK, qf, 0.0), jnp.where(lane >= D_QK, qf, 0.0)],
                           axis=0).astype(BF16)


def _diff_out(o, tq, lam_ref, gain_ref, li_ref):
    lam = lam_ref[...]
    li = li_ref[...]
    lmb = (jnp.exp(jnp.sum(lam[0:1, :] * lam[1:2, :], axis=1, keepdims=True))
           - jnp.exp(jnp.sum(lam[2:3, :] * lam[3:4, :], axis=1, keepdims=True)) + li)
    o = o[0:tq, :] - lmb * o[tq:2 * tq, :]
    o = o * lax.rsqrt(jnp.mean(o * o, axis=1, keepdims=True) + RMS_EPS)
    return o * gain_ref[...] * (1.0 - li)


def _flash_body(*refs, tq, tk, chunk, mode):
    if mode == "fox":
        q_ref, k_ref, v_ref, ck_ref = refs[:4]
        o_ref, state = refs[4], refs[5:]
    else:
        q_ref, k_ref, v_ref, lam_ref, gain_ref, li_ref = refs[:6]
        o_ref, state = refs[6], refs[7:]
    n_keys = k_ref.shape[1]
    q0 = pl.program_id(2) * tq
    n_full, n_need = _tile_counts(q0, tq, tk, chunk, n_keys)
    qlim = _row_limits(q0 + lax.broadcasted_iota(jnp.int32, (tq, 1), 0), chunk)
    if mode == "fox":
        q = q_ref[0]
    else:
        q = _split_subheads(q_ref[0])
        qlim = jnp.concatenate([qlim, qlim], axis=0)
    _softmax_init(state[0], state[1], state[3])

    def step(j, masked):
        ks = pl.multiple_of(j * tk, tk)
        s = _qk(q, k_ref[0, pl.ds(ks, tk), :])
        ck = ck_ref[0, 0, j] * LOG2E if mode == "fox" else None
        kpos = ks + lax.broadcasted_iota(jnp.int32, (1, tk), 1)

        def fix(x, r0):
            if ck is not None:
                x = x - ck
            if masked:
                x = jnp.where(kpos < qlim[r0:r0 + STRIP, :], x, NEG)
            return x

        _softmax_tile(s, _pv(v_ref[0, pl.ds(ks, tk), :]), state,
                      fix if (masked or ck is not None) else None)

    def full_step(j, c):
        step(j, False)
        return c

    def masked_step(j, c):
        step(j, True)
        return c

    lax.fori_loop(0, n_full, full_step, 0)
    lax.fori_loop(n_full, n_need, masked_step, 0)

    o = _softmax_out(state)
    if mode == "diff":
        o = _diff_out(o, tq, lam_ref, gain_ref, li_ref)
    o_ref[0] = o.astype(BF16)


def _flash(mode, q, q_off, k, k_off, v, v_off, extra, *, tq, tk):
    b, t, _ = q.shape
    heads = C_HEADS if mode == "fox" else D_HEADS
    stack = 1 if mode == "fox" else 2
    chunk = 1 if mode == "fox" else CHUNK
    in_specs = [pl.BlockSpec((1, tq, LANE), lambda bi, h, i: (bi, i, q_off + h)),
                pl.BlockSpec((1, t, LANE), lambda bi, h, i: (bi, 0, k_off + h)),
                pl.BlockSpec((1, t, LANE), lambda bi, h, i: (bi, 0, v_off + h))]
    if mode == "fox":
        in_specs.append(pl.BlockSpec((1, 1, t // tk, 1, tk), lambda bi, h, i: (bi, h, 0, 0, 0)))
    else:
        in_specs += [pl.BlockSpec(e.shape, lambda bi, h, i: (0, 0)) for e in extra]
    body = functools.partial(_flash_body, tq=tq, tk=tk, chunk=chunk, mode=mode)
    return pl.pallas_call(
        body, grid=(b, heads, t // tq), in_specs=in_specs,
        out_specs=pl.BlockSpec((1, tq, LANE), lambda bi, h, i: (bi, i, h)),
        out_shape=jax.ShapeDtypeStruct((b, t, heads * LANE), BF16),
        scratch_shapes=_softmax_scratch(stack * tq, tk),
        compiler_params=_params(3), name=mode + "_prefill")(q, k, v, *extra)


def _decode_body(*refs, t, past_len, mode):
    if mode == "fox":
        q_ref, cache_ref, kn_ref, vn_ref, ck_ref = refs[:5]
        o_ref, state = refs[5], refs[6:]
    else:
        q_ref, cache_ref, kn_ref, vn_ref, lam_ref, gain_ref, li_ref = refs[:7]
        o_ref, state = refs[7], refs[8:]
    heads = C_HEADS
    stack = 1 if mode == "fox" else 2
    rows = stack * t
    tt = lax.broadcasted_iota(jnp.int32, (t, 1), 0)
    qlim = jnp.minimum(_row_limits(past_len + tt, 1 if mode == "fox" else CHUNK), past_len + t)
    qlim = jnp.concatenate([qlim] * stack, axis=0)
    new_pos = past_len + lax.broadcasted_iota(jnp.int32, (1, LANE), 1)
    zpad = jnp.zeros((LANE - t, LANE), BF16)
    _softmax_init(state[0], state[1], state[3])

    for h in range(heads):
        hs = slice(h * LANE, (h + 1) * LANE)
        base = h * rows
        qh = q_ref[0, :, hs]
        if mode == "fox":
            kt = cache_ref[0, 0, 0:HD, h, :].astype(BF16)
            vt = cache_ref[0, 0, HD:2 * HD, h, :].astype(BF16)
            ck_past = ck_ref[0, h, :, 0:past_len] * LOG2E
            ck_new = ck_ref[0, h, :, past_len:past_len + LANE] * LOG2E
            s = jnp.dot(qh, kt, preferred_element_type=F32)
            _softmax_tile(s, lambda p, vt=vt: _qk(p, vt), state,
                          lambda x, r0, ck_past=ck_past: x - ck_past, base)
        else:
            qh = _split_subheads(qh)
            k = cache_ref[0, 0, :, h, 0:2 * D_QK].astype(BF16)
            v = cache_ref[0, 0, :, h, 2 * D_QK:].astype(BF16)
            ck_new = None
            _softmax_tile(_qk(qh, k), _pv(v), state, None, base)

        def fix(x, r0, ck_new=ck_new):
            if ck_new is not None:
                x = x - ck_new
            return jnp.where(new_pos < qlim[r0:r0 + STRIP, :], x, NEG)

        kn = jnp.concatenate([kn_ref[0, :, hs], zpad], axis=0)
        vn = jnp.concatenate([vn_ref[0, :, hs], zpad], axis=0)
        _softmax_tile(_qk(qh, kn), _pv(vn), state, fix, base)

    o = _softmax_out(state)
    for h in range(heads):
        oh = o[h * rows:(h + 1) * rows, :]
        if mode == "diff":
            oh = _diff_out(oh, t, lam_ref, gain_ref, li_ref)
        o_ref[0, :, h * LANE:(h + 1) * LANE] = oh.astype(BF16)


def _decode(mode, layer, q, q_blk, cache, k_new, k_blk, v_new, v_blk, extra, *, past_len):
    b, t, _ = q.shape
    heads = C_HEADS
    assert t <= LANE and t % STRIP == 0 and past_len % LANE == 0
    stack = 1 if mode == "fox" else 2
    wide = lambda blk: pl.BlockSpec((1, t, heads * LANE), lambda bi: (bi, 0, blk))
    in_specs = [wide(q_blk), pl.BlockSpec((1, 1) + cache.shape[2:], lambda bi: (layer, bi, 0, 0, 0)),
                wide(k_blk), wide(v_blk)]
    if mode == "fox":
        in_specs.append(pl.BlockSpec((1,) + extra[0].shape[1:], lambda bi: (bi, 0, 0, 0)))
    else:
        in_specs += [pl.BlockSpec(e.shape, lambda bi: (0, 0)) for e in extra]
    body = functools.partial(_decode_body, t=t, past_len=past_len, mode=mode)
    return pl.pallas_call(
        body, grid=(b,), in_specs=in_specs, out_specs=wide(0),
        out_shape=jax.ShapeDtypeStruct((b, t, heads * LANE), BF16),
        scratch_shapes=_softmax_scratch(stack * heads * t, past_len),
        compiler_params=_params(1), name=mode + "_decode")(q, cache, k_new, v_new, *extra)


def _dsa_body(*refs, tq, tk, past_len, n_keys, k_sel, idx_bits, cached):
    if cached:
        (q_ref, iq_ref, iw_ref, cache_ref, knt_ref, vnt_ref, iknt_ref, o_ref,
         keys, bias_sc, kt_all, vt_all, ikt_all) = refs[:13]
        state = refs[13:]
        for dst, new, lo, hi in ((kt_all, knt_ref, 0, HD), (vt_all, vnt_ref, HD, 2 * HD),
                                 (ikt_all, iknt_ref, 2 * HD, 2 * HD + IDX_HD)):
            dst[:, 0:past_len] = cache_ref[0, 0, lo:hi, :].astype(BF16)
            dst[:, past_len:] = new[0]
        index_logits = lambda iq, ks: jnp.dot(iq[:, 0:IDX_HD], ikt_all[...], preferred_element_type=F32)
        key_logits = lambda q, ks: jnp.dot(q, kt_all[...], preferred_element_type=F32)
        values = lambda ks: (lambda p: _qk(p, vt_all[...]))
    else:
        q_ref, iq_ref, iw_ref, k_ref, v_ref, ik_ref, o_ref, keys, bias_sc = refs[:9]
        state = refs[9:]
        index_logits = lambda iq, ks: _qk(iq, ik_ref[0, pl.ds(ks, tk), :])
        key_logits = lambda q, ks: _qk(q, k_ref[0, pl.ds(ks, tk), :])
        values = lambda ks: _pv(v_ref[0, pl.ds(ks, tk), :])

    q0 = past_len + pl.program_id(1) * tq
    _, n_need = _tile_counts(q0, tq, tk, CHUNK, n_keys)
    qlim = jnp.minimum(_row_limits(q0 + lax.broadcasted_iota(jnp.int32, (tq, 1), 0), CHUNK), n_keys)

    iqf = iq_ref[0]
    parts = []
    for h in range(IDX_HEADS):
        blk = iqf[:, (h // 2) * LANE:(h // 2 + 1) * LANE]
        parts.append(pltpu.roll(blk, IDX_HD, 1) if h % 2 else blk)
    iq = jnp.concatenate(parts, axis=0).astype(BF16)
    iw = iw_ref[0]
    w_cols = [iw[:, IW_LANE + h:IW_LANE + h + 1] for h in range(IDX_HEADS)]

    def score_tile(j, c):
        ks = pl.multiple_of(j * tk, tk)
        rel = index_logits(iq, ks)
        sc = jnp.zeros((tq, tk), F32)
        for h in range(IDX_HEADS):
            sc = sc + jnp.maximum(rel[h * tq:(h + 1) * tq, :], 0.0) * w_cols[h]
        sc = sc * IDX_SCALE
        sc = jnp.where(sc == 0.0, 0.0, sc)
        bits = lax.bitcast_convert_type(sc, jnp.int32)
        key = jnp.where(bits < 0, bits ^ jnp.int32(0x7FFFFFFF), bits)
        kpos = ks + lax.broadcasted_iota(jnp.int32, (1, tk), 1)
        keys[j] = jnp.where(kpos < qlim, key, INT_MIN)
        return c

    lax.fori_loop(0, n_need, score_tile, 0)

    def count(pred):
        def body(j, acc):
            hit = jnp.where(pred(keys[j], j * tk), 1.0, 0.0)
            for c in range(tk // LANE):
                acc = acc + hit[:, c * LANE:(c + 1) * LANE]
            return acc
        acc = lax.fori_loop(0, n_need, body, jnp.zeros((tq, LANE), F32))
        return jnp.sum(acc, axis=1, keepdims=True)

    def count_ge(cand):
        return count(lambda t, _: t >= cand)

    def any_row(flag):
        return (jnp.max(jnp.where(flag, 1.0, 0.0)) > 0.0).astype(jnp.int32)

    want = jnp.float32(k_sel)
    crowded = count_ge(jnp.full((tq, 1), INT_MIN + 1, jnp.int32)) > want

    def search_on(st):
        return (st[0] < 32) & (st[1] > 0)

    def value_bit(st):
        b, _, ans, n_ans = st
        cand = ans + lax.shift_left(jnp.int32(1), 31 - b)
        n = count_ge(cand)
        take = n >= want
        ans = jnp.where(take, cand, ans)
        n_ans = jnp.where(take, n, n_ans)
        return b + 1, any_row(crowded & (n_ans != want)), ans, n_ans

    start = (jnp.int32(0), any_row(crowded), jnp.full((tq, 1), INT_MIN, jnp.int32),
             jnp.full((tq, 1), 2.0 * k_sel + 1.0, F32))
    _, _, kstar, n_ge = lax.while_loop(search_on, value_bit, start)
    tie = crowded & (n_ge > want)

    @pl.when(any_row(tie) > 0)
    def _():
        need = want - count_ge(kstar + 1)

        def pos_of(base):
            return base + lax.broadcasted_iota(jnp.int32, (1, tk), 1)

        def index_bit(b, ans):
            cand = ans + lax.shift_left(jnp.int32(1), idx_bits - 1 - b)
            n_before = count(lambda t, base: (t == kstar) & (pos_of(base) < cand))
            return jnp.where(n_before < need, cand, ans)

        last = lax.fori_loop(0, idx_bits, index_bit, jnp.zeros((tq, 1), jnp.int32))

        def drop(j, c):
            t = keys[j]
            keys[j] = jnp.where(tie & (t == kstar) & (pos_of(j * tk) > last), INT_MIN, t)
            return c

        lax.fori_loop(0, n_need, drop, 0)

    thr = jnp.maximum(kstar, INT_MIN + 1)
    qf = q_ref[0]
    q = jnp.concatenate([qf[:, h * LANE:(h + 1) * LANE] for h in range(B_HEADS)], axis=0)
    _softmax_init(state[0], state[1], state[3])

    def attend(j, c):
        ks = pl.multiple_of(j * tk, tk)
        bias_sc[...] = jnp.where(keys[j] >= thr, 0.0, NEG)
        _softmax_tile(key_logits(q, ks), values(ks), state,
                      lambda x, r0: x + bias_sc[r0 % tq:r0 % tq + STRIP, :])
        return c

    lax.fori_loop(0, n_need, attend, 0)
    o = _softmax_out(state)
    for h in range(B_HEADS):
        o_ref[0, :, h * LANE:(h + 1) * LANE] = o[h * tq:(h + 1) * tq, :].astype(BF16)


def _dsa(q, iq, iw, kv, *, layer, past_len, n_keys, tq, tk):
    b, t, _ = q.shape
    cached = past_len > 0
    lpad = -(-n_keys // tk) * tk
    assert not cached or (lpad == tk and past_len % LANE == 0)
    k_sel = min(TOPK_MAX, n_keys // 4)
    body = functools.partial(_dsa_body, tq=tq, tk=tk, past_len=past_len, n_keys=n_keys, k_sel=k_sel,
                             idx_bits=max(1, (lpad - 1).bit_length()), cached=cached)
    rows = B_HEADS * tq
    in_specs = [pl.BlockSpec((1, tq, B_HEADS * LANE), lambda bi, i: (bi, i, R_BQ // B_HEADS)),
                pl.BlockSpec((1, tq, 2 * LANE), lambda bi, i: (bi, i, R_BIQ // 2)),
                pl.BlockSpec((1, tq, LANE), lambda bi, i: (bi, i, P_SMALL))]
    scratch = [pltpu.VMEM((lpad // tk, tq, tk), jnp.int32), pltpu.VMEM((tq, tk), F32)]
    if cached:
        cache = kv[0]
        in_specs.append(pl.BlockSpec((1, 1) + cache.shape[2:], lambda bi, i: (layer, bi, 0, 0)))
        in_specs += [pl.BlockSpec((1,) + a.shape[1:], lambda bi, i: (bi, 0, 0)) for a in kv[1:]]
        args = kv
        scratch += [pltpu.VMEM((a.shape[1], lpad), BF16) for a in kv[1:]]
    else:
        k, k_blk, v, v_blk, ik, ik_blk = kv
        in_specs += [pl.BlockSpec((1, n_keys, LANE), lambda bi, i: (bi, 0, k_blk)),
                     pl.BlockSpec((1, n_keys, LANE), lambda bi, i: (bi, 0, v_blk)),
                     pl.BlockSpec((1, n_keys, LANE), lambda bi, i: (bi, 0, ik_blk))]
        args = (k, v, ik)
    return pl.pallas_call(
        body, grid=(b, t // tq), in_specs=in_specs,
        out_specs=pl.BlockSpec((1, tq, B_HEADS * LANE), lambda bi, i: (bi, i, 0)),
        out_shape=jax.ShapeDtypeStruct((b, t, B_HEADS * LANE), BF16),
        scratch_shapes=scratch + _softmax_scratch(rows, tk),
        compiler_params=_params(2), name="dsa_decode" if cached else "dsa_prefill")(q, iq, iw, *args)


def _merge_body(ya_ref, yb_ref, yc_ref, yd_ref, g_ref, wb_ref, o_ref):
    d = o_ref.shape[1]
    acc = 0.0
    for m, y_ref in enumerate((ya_ref, yb_ref, yc_ref, yd_ref)):
        acc = acc + g_ref[:, m * d:(m + 1) * d].astype(F32) * jnp.dot(
            y_ref[...], wb_ref[m], preferred_element_type=F32)
    o_ref[...] = acc.astype(BF16)


def _merge(ys, gate, w_branch):
    m, d = gate.shape[0], w_branch.shape[2]
    tm = _tile(m, 256)
    y_spec = pl.BlockSpec((tm, BR_W), lambda i: (i, 0))
    return pl.pallas_call(
        _merge_body, grid=(m // tm,),
        in_specs=[y_spec] * N_BRANCH + [pl.BlockSpec((tm, N_BRANCH * d), lambda i: (i, 0)),
                                        pl.BlockSpec(w_branch.shape, lambda i: (0, 0, 0))],
        out_specs=pl.BlockSpec((tm, d), lambda i: (i, 0)),
        out_shape=jax.ShapeDtypeStruct((m, d), BF16),
        compiler_params=_params(1), name="merge")(*ys, gate, w_branch)


def _layernorm(z, g, b):
    mu = jnp.mean(z, axis=-1, keepdims=True)
    zc = z - mu
    var = jnp.mean(zc * zc, axis=-1, keepdims=True)
    return zc * lax.rsqrt(var + LN_EPS) * g + b


def _wo_ln_body(m_ref, x_ref, w_ref, g_ref, b_ref, of_ref, ob_ref, *, alpha):
    h = jnp.dot(m_ref[...], w_ref[...], preferred_element_type=F32)
    y = _layernorm(alpha * x_ref[...] + h, g_ref[...], b_ref[...])
    of_ref[...] = y
    ob_ref[...] = y.astype(BF16)


def _wo_ln(merged, x, w_o, g, b, alpha):
    m, d = x.shape
    tm = _tile(m, 256)
    row = pl.BlockSpec((tm, d), lambda i: (i, 0))
    vec = pl.BlockSpec((1, d), lambda i: (0, 0))
    return pl.pallas_call(
        functools.partial(_wo_ln_body, alpha=alpha), grid=(m // tm,),
        in_specs=[row, row, pl.BlockSpec((d, d), lambda i: (0, 0)), vec, vec],
        out_specs=[row, row],
        out_shape=[jax.ShapeDtypeStruct((m, d), F32), jax.ShapeDtypeStruct((m, d), BF16)],
        compiler_params=_params(1), name="wo_ln")(merged, x, w_o, g, b)


def _ffn_body(xb_ref, xf_ref, wu_ref, wd_ref, g_ref, b_ref, of_ref, ob_ref, acc, *, alpha):
    f = pl.program_id(1)

    @pl.when(f == 0)
    def _():
        acc[...] = jnp.zeros_like(acc)

    h = jnp.maximum(jnp.dot(xb_ref[...], wu_ref[...], preferred_element_type=F32), 0.0)
    acc[...] += jnp.dot((h * h).astype(BF16), wd_ref[...], preferred_element_type=F32)

    @pl.when(f == pl.num_programs(1) - 1)
    def _():
        y = _layernorm(alpha * xf_ref[...] + acc[...], g_ref[...], b_ref[...])
        of_ref[...] = y
        ob_ref[...] = y.astype(BF16)


def _ffn(xb, xf, w_up, w_down, g, b, alpha):
    m, d = xf.shape
    dff = w_up.shape[1]
    tm = _tile(m, 512)
    tf = _tile(dff, 512, LANE)
    row = pl.BlockSpec((tm, d), lambda i, f: (i, 0))
    vec = pl.BlockSpec((1, d), lambda i, f: (0, 0))
    return pl.pallas_call(
        functools.partial(_ffn_body, alpha=alpha), grid=(m // tm, dff // tf),
        in_specs=[row, row, pl.BlockSpec((d, tf), lambda i, f: (0, f)),
                  pl.BlockSpec((tf, d), lambda i, f: (f, 0)), vec, vec],
        out_specs=[row, row],
        out_shape=[jax.ShapeDtypeStruct((m, d), F32), jax.ShapeDtypeStruct((m, d), BF16)],
        scratch_shapes=[pltpu.VMEM((tm, d), F32)],
        compiler_params=_params(2), name="ffn")(xb, xf, w_up, w_down, g, b)


def _prep_weights(w_in, w_branch, w_o, w_up, w_down):
    d = w_in.shape[1]
    sizes = (BR_W, BR_W, BR_W, B_HEADS * HD, HD, HD, IDX_HEADS * IDX_HD, IDX_HD, IDX_HEADS,
             C_HEADS * HD, C_HEADS * HD, C_HEADS * HD, C_HEADS,
             D_HEADS * 2 * D_QK, D_HEADS * 2 * D_QK, D_HEADS * D_V, N_BRANCH * d)
    names = ("a_b", "a_c", "a_h", "b_q", "b_k", "b_v", "b_iq", "b_ik", "b_iw",
             "c_q", "c_k", "c_v", "c_f", "d_q", "d_k", "d_v", "gate")
    starts = np.concatenate([[0], np.cumsum(sizes)])
    col = {n: w_in[:, :, int(starts[i]):int(starts[i + 1])] for i, n in enumerate(names)}
    depth = w_in.shape[0]

    def zeros(n):
        return jnp.zeros((depth, d, n), w_in.dtype)

    w_a = jnp.concatenate([col["a_b"], col["a_c"], col["a_h"]], axis=2)
    w_r = jnp.concatenate([col["b_q"], col["b_iq"], col["b_k"], col["b_ik"], zeros(LANE - IDX_HD),
                           col["d_q"], col["d_k"]], axis=2)
    w_p = jnp.concatenate([col["c_q"], col["c_k"], col["c_v"], col["d_v"], col["b_v"],
                           col["b_iw"], col["c_f"], zeros(LANE - IDX_HEADS - C_HEADS)], axis=2)
    assert w_r.shape[2] == R_BLOCKS * LANE and w_p.shape[2] == P_BLOCKS * LANE
    cast = lambda w: w.astype(BF16)
    return dict(w_a=cast(w_a), w_r=cast(w_r), w_p=cast(w_p), w_g=cast(col["gate"]),
                w_branch=cast(w_branch), w_o=cast(w_o), w_up=cast(w_up), w_down=cast(w_down))


def _cols(a, block, n_blocks=1):
    return a[:, :, block * LANE:(block + n_blocks) * LANE]


def _trunk(x, past_len, conv_state, dsa_rows, fox_rows, diff_rows, wts, conv_w, fox_fbias,
           diff_lambda, diff_gain, ln1_g, ln1_b, ln2_g, ln2_b):
    b, t, d = x.shape
    depth = conv_w.shape[0]
    alpha = (2 * depth) ** 0.25
    m = b * t
    n_keys = past_len + t
    pos = jnp.tile(past_len + jnp.arange(t, dtype=jnp.int32), b)
    tab = _rope_tables(pos)
    if past_len:
        fox_cache = jnp.transpose(fox_rows, (0, 1, 4, 3, 2))
        dsa_cache = jnp.transpose(dsa_rows, (0, 1, 3, 2))
        tq_dsa, tk_dsa = t, -(-n_keys // LANE) * LANE
        new_pad = tk_dsa - past_len

        def new_t(a):
            return jnp.pad(jnp.transpose(a, (0, 2, 1)), ((0, 0), (0, 0), (0, new_pad - t))).astype(BF16)
    else:
        tq_att = _tile(t, 512)
        tk_att = _tile(t, 512, LANE)
        tq_dsa, tk_dsa = _tile(t, 128), tk_att

    xf = x.reshape(m, d)
    xb = xf.astype(BF16)
    conv_out, dsa_out, fox_out, diff_out = [], [], [], []
    for l in range(depth):
        lam_init = 0.8 - 0.6 * float(np.exp(-0.3 * l))
        u_a = _matmul("proj_conv", xb, wts["w_a"][l])[0].reshape(b, t, 3 * BR_W)
        rf, rb = _matmul("proj_rope", xb, wts["w_r"][l], tab=tab, kinds=R_KINDS, scales=R_SCALES,
                         want_b16=True)
        pf, pb = _matmul("proj_plain", xb, wts["w_p"][l], kinds=P_KINDS, scales=P_SCALES, want_b16=True)
        (gate,) = _matmul("proj_gate", xb, wts["w_g"][l], sigmoid=True, want_f32=False, want_b16=True,
                          tm_target=512)
        rf, rb, pf, pb = (a.reshape(b, t, a.shape[1]) for a in (rf, rb, pf, pb))

        y_a, n_conv = _short_conv(u_a, conv_state[l], conv_w[l])

        small = _cols(pf, P_SMALL)
        bias_row = jnp.zeros((1, LANE), F32).at[0, CF_LANE:CF_LANE + C_HEADS].set(fox_fbias[l])
        if past_len:
            past_lf = jnp.transpose(fox_cache[l, :, 2 * HD], (0, 2, 1))
            past_lf = jnp.pad(past_lf, ((0, 0), (0, 0), (CF_LANE, LANE - CF_LANE - C_HEADS)))
            small = jnp.concatenate([past_lf, small], axis=1)
        lf, cum = _fox_prep(small, bias_row, past_len)
        lf_new = lf[:, past_len:, CF_LANE:CF_LANE + C_HEADS]
        cum = cum[:, :, CF_LANE:CF_LANE + C_HEADS]
        diff_extra = (diff_lambda[l], diff_gain[l].reshape(1, D_V), jnp.full((1, 1), lam_init, F32))
        if past_len:
            ck = jnp.pad(jnp.transpose(cum, (0, 2, 1)), ((0, 0), (0, 0), (0, past_len + LANE - n_keys)))
            y_c = _decode("fox", l, pb, P_CQ // 4, fox_cache, pb, P_CK // 4, pb, P_CV // 4,
                          (ck[:, :, None, :],), past_len=past_len)
            y_d = _decode("diff", l, rb, R_DQ // 4, diff_rows, rb, R_DK // 4, pb, P_DV // 4,
                          diff_extra, past_len=past_len)
            dsa_kv = (dsa_cache, new_t(_cols(rf, R_BK)), new_t(_cols(pf, P_BV)),
                      new_t(_cols(rf, R_BIK)[..., :IDX_HD]))
        else:
            ck = jnp.transpose(cum, (0, 2, 1)).reshape(b, C_HEADS, t // tk_att, 1, tk_att)
            y_c = _flash("fox", pb, P_CQ, pb, P_CK, pb, P_CV, (ck,), tq=tq_att, tk=tk_att)
            y_d = _flash("diff", rb, R_DQ, rb, R_DK, pb, P_DV, diff_extra, tq=tq_att, tk=tk_att)
            dsa_kv = (rb, R_BK, pb, P_BV, rb, R_BIK)
        y_b = _dsa(rb, rf, pf, dsa_kv, layer=l, past_len=past_len, n_keys=n_keys, tq=tq_dsa, tk=tk_dsa)

        ys = [y.reshape(m, BR_W) for y in (y_a, y_b, y_c, y_d)]
        merged = _merge(ys, gate, wts["w_branch"][l])
        xf, xb = _wo_ln(merged, xf, wts["w_o"][l], ln1_g[l].reshape(1, d), ln1_b[l].reshape(1, d), alpha)
        xf, xb = _ffn(xb, xf, wts["w_up"][l], wts["w_down"][l], ln2_g[l].reshape(1, d),
                      ln2_b[l].reshape(1, d), alpha)

        conv_out.append(n_conv)
        dsa_out.append(jnp.concatenate([_cols(rf, R_BK), _cols(pf, P_BV), _cols(rf, R_BIK)[..., :IDX_HD]],
                                       axis=-1))
        fox_out.append(jnp.concatenate([_cols(pf, P_CK, 4).reshape(b, t, C_HEADS, HD),
                                        _cols(pf, P_CV, 4).reshape(b, t, C_HEADS, HD),
                                        lf_new[..., None]], axis=-1))
        diff_out.append(jnp.concatenate([_cols(rf, R_DK, 4).reshape(b, t, D_HEADS, 2 * D_QK),
                                         _cols(pf, P_DV, 4).reshape(b, t, D_HEADS, D_V)], axis=-1))
    return (xf.reshape(b, t, d), jnp.stack(conv_out), jnp.stack(dsa_out), jnp.stack(fox_out),
            jnp.stack(diff_out))


def kernel(x_prompt, x_sample, state_conv, cache_dsa, cache_fox, cache_diff, w_in, conv_w, fox_fbias,
           diff_lambda, diff_gain, w_branch, w_o, ln1_g, ln1_b, w_up, w_down, ln2_g, ln2_b):
    depth = w_in.shape[0]
    b = x_prompt.shape[0]
    wts = _prep_weights(w_in, w_branch, w_o, w_up, w_down)
    shared = (wts, conv_w, fox_fbias, diff_lambda, diff_gain, ln1_g, ln1_b, ln2_g, ln2_b)
    conv0 = jnp.zeros((depth, b, CONV_W - 1, BR_W), x_prompt.dtype)
    y_p, conv_p, dsa_p, fox_p, diff_p = _trunk(x_prompt, 0, conv0, None, None, None, *shared)
    y_s, conv_s, dsa_s, fox_s, diff_s = _trunk(x_sample, cache_dsa.shape[2], state_conv, cache_dsa,
                                               cache_fox, cache_diff, *shared)
    return (y_p, y_s, conv_p, dsa_p, fox_p, diff_p, conv_s, dsa_s, fox_s, diff_s)
```

```python
import functools

import numpy as np
import jax
import jax.numpy as jnp
from jax import lax
from jax.experimental import pallas as pl
from jax.experimental.pallas import tpu as pltpu

CHUNK = 64
N_BRANCH = 4
BR_W = 512
HD = 128
CONV_W = 3
B_HEADS = 4
IDX_HEADS = 4
IDX_HD = 64
TOPK_MAX = 256
IDX_SCALE = (IDX_HEADS * IDX_HD) ** -0.5
C_HEADS = 4
D_HEADS = 4
D_QK = 64
D_V = 128
ROPE_THETA = 500000.0
LN_EPS = 1e-5
RMS_EPS = 1e-5

LANE = 128
STRIP = 16
VMEM_LIMIT = 56 * 1024 * 1024
NEG = -1e30
LOG2E = 1.4426950408889634
INT_MIN = np.int32(-2 ** 31)
F32 = jnp.float32
BF16 = jnp.bfloat16

R_BQ, R_BIQ, R_BK, R_BIK, R_DQ, R_DK, R_BLOCKS = 0, 4, 6, 7, 8, 12, 16
P_CQ, P_CK, P_CV, P_DV, P_BV, P_SMALL, P_BLOCKS = 0, 4, 8, 12, 16, 17, 18
IW_LANE, CF_LANE = 0, IDX_HEADS
R_KINDS = ("r128",) * 4 + ("r64",) * 2 + ("r128", "r64") + ("r64",) * 8
R_SCALES = (HD ** -0.5 * LOG2E,) * 4 + (1.0,) * 4 + (D_QK ** -0.5 * LOG2E,) * 4 + (1.0,) * 4
P_KINDS = (None,) * P_BLOCKS
P_SCALES = (HD ** -0.5 * LOG2E,) * 4 + (1.0,) * 14


def _tile(n, target, align=8):
    if n <= target:
        return n
    for t in range(target - target % align, 0, -align):
        if n % t == 0:
            return t
    return n


def _params(n_grid):
    return pltpu.CompilerParams(dimension_semantics=("arbitrary",) * n_grid,
                                vmem_limit_bytes=VMEM_LIMIT)


def _mm_body(x_ref, w_ref, *rest, kinds, scales, has_tab, sigmoid, want_f32, want_b16):
    tab_ref = rest[0] if has_tab else None
    outs = rest[1:] if has_tab else rest
    of_ref = outs[0] if want_f32 else None
    ob_ref = outs[-1] if want_b16 else None
    acc = jnp.dot(x_ref[...], w_ref[...], preferred_element_type=F32)
    if kinds is None:
        if sigmoid:
            acc = jax.nn.sigmoid(acc)
        if want_f32:
            of_ref[...] = acc
        if want_b16:
            ob_ref[...] = acc.astype(BF16)
        return
    for c, (kind, scale) in enumerate(zip(kinds, scales)):
        sl = slice(c * LANE, (c + 1) * LANE)
        blk = acc[:, sl]
        if kind is not None:
            base = 0 if kind == "r128" else 3 * LANE
            half = (HD // 4) // 2 if kind == "r128" else (IDX_HD // 4) // 2
            cos = tab_ref[:, base:base + LANE]
            s_lo = tab_ref[:, base + LANE:base + 2 * LANE]
            s_hi = tab_ref[:, base + 2 * LANE:base + 3 * LANE]
            blk = (blk * cos + pltpu.roll(blk, LANE - half, 1) * s_lo
                   + pltpu.roll(blk, half, 1) * s_hi)
        if want_f32:
            of_ref[:, sl] = blk
        if want_b16:
            ob_ref[:, sl] = (blk * scale if scale != 1.0 else blk).astype(BF16)


def _matmul(name, x, w, *, tab=None, kinds=None, scales=None, sigmoid=False,
            want_f32=True, want_b16=False, tm_target=256, tn_target=2048):
    m, k = x.shape
    n = w.shape[1]
    tm = _tile(m, tm_target)
    tn = n if kinds is not None else _tile(n, tn_target, LANE)
    grid = (n // tn, m // tm)
    in_specs = [pl.BlockSpec((tm, k), lambda j, i: (i, 0)),
                pl.BlockSpec((k, tn), lambda j, i: (0, j))]
    args = [x, w]
    if tab is not None:
        in_specs.append(pl.BlockSpec((tm, tab.shape[1]), lambda j, i: (i, 0)))
        args.append(tab)
    out_shape, out_specs = [], []
    if want_f32:
        out_shape.append(jax.ShapeDtypeStruct((m, n), F32))
        out_specs.append(pl.BlockSpec((tm, tn), lambda j, i: (i, j)))
    if want_b16:
        out_shape.append(jax.ShapeDtypeStruct((m, n), BF16))
        out_specs.append(pl.BlockSpec((tm, tn), lambda j, i: (i, j)))
    body = functools.partial(_mm_body, kinds=kinds, scales=scales, has_tab=tab is not None,
                             sigmoid=sigmoid, want_f32=want_f32, want_b16=want_b16)
    return pl.pallas_call(body, grid=grid, in_specs=in_specs, out_specs=out_specs,
                          out_shape=out_shape, compiler_params=_params(2), name=name)(*args)


def _rope_tables(pos):
    tabs = []
    for hd in (HD, IDX_HD):
        rot = hd // 4
        half = rot // 2
        freqs = ROPE_THETA ** (-2.0 * jnp.arange(half, dtype=F32) / rot)
        ang = pos.astype(F32)[:, None] * freqs[None, :]
        cos, sin = jnp.cos(ang), jnp.sin(ang)
        m = pos.shape[0]
        one = jnp.ones((m, hd - rot), F32)
        zero_h = jnp.zeros((m, half), F32)
        zero_t = jnp.zeros((m, hd - rot), F32)
        c = jnp.concatenate([cos, cos, one], axis=1)
        s_lo = jnp.concatenate([-sin, zero_h, zero_t], axis=1)
        s_hi = jnp.concatenate([zero_h, sin, zero_t], axis=1)
        rep = LANE // hd
        tabs += [jnp.tile(c, (1, rep)), jnp.tile(s_lo, (1, rep)), jnp.tile(s_hi, (1, rep))]
    return jnp.concatenate(tabs, axis=1)


def _conv_body(u_ref, st_ref, w_ref, y_ref, ns_ref, carry, *, tm):
    @pl.when(pl.program_id(1) == 0)
    def _():
        carry[...] = st_ref[0]

    gate_b = u_ref[0, :, 0:BR_W]
    v = u_ref[0, :, BR_W:2 * BR_W] * u_ref[0, :, 2 * BR_W:3 * BR_W]
    p0 = carry[0:1, :]
    p1 = carry[1:2, :]
    row = lax.broadcasted_iota(jnp.int32, (tm, BR_W), 0)
    v1 = jnp.where(row == 0, p1, pltpu.roll(v, 1, 0))
    v2 = jnp.where(row == 0, p0, jnp.where(row == 1, p1, pltpu.roll(v, 2, 0)))
    w = w_ref[...]
    y = gate_b * (v2 * w[0:1, :] + v1 * w[1:2, :] + v * w[2:3, :])
    y_ref[0] = y.astype(BF16)
    tail = (u_ref[0, tm - 2:tm, BR_W:2 * BR_W] * u_ref[0, tm - 2:tm, 2 * BR_W:3 * BR_W])
    carry[...] = tail
    ns_ref[0] = tail


def _short_conv(u_a, state, conv_w):
    b, t, _ = u_a.shape
    assert t >= CONV_W - 1
    tm = _tile(t, 512)
    body = functools.partial(_conv_body, tm=tm)
    return pl.pallas_call(
        body, grid=(b, t // tm),
        in_specs=[pl.BlockSpec((1, tm, 3 * BR_W), lambda bi, i: (bi, i, 0)),
                  pl.BlockSpec((1, CONV_W - 1, BR_W), lambda bi, i: (bi, 0, 0)),
                  pl.BlockSpec((CONV_W, BR_W), lambda bi, i: (0, 0))],
        out_specs=[pl.BlockSpec((1, tm, BR_W), lambda bi, i: (bi, i, 0)),
                   pl.BlockSpec((1, CONV_W - 1, BR_W), lambda bi, i: (bi, 0, 0))],
        out_shape=[jax.ShapeDtypeStruct((b, t, BR_W), BF16),
                   jax.ShapeDtypeStruct((b, CONV_W - 1, BR_W), F32)],
        scratch_shapes=[pltpu.VMEM((CONV_W - 1, BR_W), F32)],
        compiler_params=_params(2), name="short_conv")(u_a, state, conv_w)


def _fox_prep_body(x_ref, bias_ref, lf_ref, cum_ref, carry, *, tl, past_len):
    i = pl.program_id(1)

    @pl.when(i == 0)
    def _():
        carry[...] = jnp.zeros_like(carry)

    x = x_ref[0]
    z = x + bias_ref[...]
    log_sig = jnp.minimum(z, 0.0) - jnp.log1p(jnp.exp(-jnp.abs(z)))
    row = i * tl + lax.broadcasted_iota(jnp.int32, (tl, 1), 0)
    lf = jnp.where(row >= past_len, log_sig, x)
    r = lax.broadcasted_iota(jnp.int32, (tl, tl), 0)
    c = lax.broadcasted_iota(jnp.int32, (tl, tl), 1)
    tri = jnp.where(c <= r, 1.0, 0.0).astype(F32)
    cs = jnp.dot(tri, lf, precision=lax.Precision.HIGHEST, preferred_element_type=F32) + carry[...]
    lf_ref[0] = lf
    cum_ref[0] = cs
    carry[...] = cs[tl - 1:tl, :]


def _fox_prep(x, bias_row, past_len):
    b, l, _ = x.shape
    tl = _tile(l, 512)
    body = functools.partial(_fox_prep_body, tl=tl, past_len=past_len)
    spec = pl.BlockSpec((1, tl, LANE), lambda bi, i: (bi, i, 0))
    return pl.pallas_call(
        body, grid=(b, l // tl),
        in_specs=[spec, pl.BlockSpec((1, LANE), lambda bi, i: (0, 0))],
        out_specs=[spec, spec],
        out_shape=[jax.ShapeDtypeStruct((b, l, LANE), F32)] * 2,
        scratch_shapes=[pltpu.VMEM((1, LANE), F32)],
        compiler_params=_params(2), name="fox_prep")(x, bias_row)


def _row_limits(qpos, chunk):
    if chunk == 1:
        return qpos + 1
    sh = chunk.bit_length() - 1
    return lax.shift_left(lax.shift_right_arithmetic(qpos, sh) + 1, sh)


def _tile_counts(q0, tq, tk, chunk, n_keys):
    lim_min = (q0 // chunk + 1) * chunk
    lim_max = ((q0 + tq - 1) // chunk + 1) * chunk
    n_full = jnp.minimum(lim_min, n_keys) // tk
    n_need = (jnp.minimum(lim_max, n_keys) + tk - 1) // tk
    return n_full, n_need


def _softmax_scratch(rows, tk):
    return [pltpu.VMEM((rows, LANE), F32)] * 4 + [pltpu.VMEM((2, rows, tk), BF16)]


def _softmax_init(m_sc, l_sc, acc_sc):
    m_sc[...] = jnp.full(m_sc.shape, NEG, F32)
    l_sc[...] = jnp.zeros(l_sc.shape, F32)
    acc_sc[...] = jnp.zeros(acc_sc.shape, F32)


def _softmax_tile(s, pv, state, fix=None, base=0, slot=0):
    m_sc, l_sc, a_sc, acc_sc, p_sc = state
    rows, tk = s.shape
    for r0 in range(0, rows, STRIP):
        rs = slice(base + r0, base + r0 + STRIP)
        x = s[r0:r0 + STRIP, :]
        if fix is not None:
            x = fix(x, r0)
        blocks = [x[:, c * LANE:(c + 1) * LANE] for c in range(tk // LANE)]
        mx = blocks[0]
        for blk in blocks[1:]:
            mx = jnp.maximum(mx, blk)
        m_old = m_sc[rs, :]
        m_new = jnp.maximum(m_old, jnp.max(mx, axis=1, keepdims=True))
        alpha = jnp.exp2(m_old - m_new)
        tot = None
        for c, blk in enumerate(blocks):
            p = jnp.exp2(blk - m_new)
            tot = p if tot is None else tot + p
            p_sc[slot, rs, c * LANE:(c + 1) * LANE] = p.astype(BF16)
        l_sc[rs, :] = alpha * l_sc[rs, :] + tot
        m_sc[rs, :] = m_new
        a_sc[rs, :] = alpha
    rs = slice(base, base + rows)
    acc_sc[rs, :] = a_sc[rs, :] * acc_sc[rs, :] + pv(p_sc[slot, rs, 0:tk])


def _pv(v):
    return lambda p: jnp.dot(p, v, preferred_element_type=F32)


def _softmax_out(state):
    _, l_sc, _, acc_sc, _ = state
    return acc_sc[...] / jnp.sum(l_sc[...], axis=1, keepdims=True)


def _qk(q, k):
    return lax.dot_general(q, k, (((1,), (1,)), ((), ())), preferred_element_type=F32)


def _split_subheads(q):
    qf = q.astype(F32)
    lane = lax.broadcasted_iota(jnp.int32, qf.shape, 1)
    return jnp.concatenate([jnp.where(lane < D_QK, qf, 0.0), jnp.where(lane >= D_QK, qf, 0.0)],
                           axis=0).astype(BF16)


def _diff_out(o, tq, lam_ref, gain_ref, li_ref):
    lam = lam_ref[...]
    li = li_ref[...]
    lmb = (jnp.exp(jnp.sum(lam[0:1, :] * lam[1:2, :], axis=1, keepdims=True))
           - jnp.exp(jnp.sum(lam[2:3, :] * lam[3:4, :], axis=1, keepdims=True)) + li)
    o = o[0:tq, :] - lmb * o[tq:2 * tq, :]
    o = o * lax.rsqrt(jnp.mean(o * o, axis=1, keepdims=True) + RMS_EPS)
    return o * gain_ref[...] * (1.0 - li)


def _flash_body(*refs, tq, tk, chunk, mode):
    if mode == "fox":
        q_ref, k_ref, v_ref, ck_ref = refs[:4]
        o_ref, state = refs[4], refs[5:]
    else:
        q_ref, k_ref, v_ref, lam_ref, gain_ref, li_ref = refs[:6]
        o_ref, state = refs[6], refs[7:]
    n_keys = k_ref.shape[1]
    q0 = pl.program_id(2) * tq
    n_full, n_need = _tile_counts(q0, tq, tk, chunk, n_keys)
    qlim = _row_limits(q0 + lax.broadcasted_iota(jnp.int32, (tq, 1), 0), chunk)
    if mode == "fox":
        q = q_ref[0]
    else:
        q = _split_subheads(q_ref[0])
        qlim = jnp.concatenate([qlim, qlim], axis=0)
    _softmax_init(state[0], state[1], state[3])

    def step(j, masked, slot):
        ks = pl.multiple_of(j * tk, tk)
        s = _qk(q, k_ref[0, pl.ds(ks, tk), :])
        ck = ck_ref[0, 0, j] * LOG2E if mode == "fox" else None
        kpos = ks + lax.broadcasted_iota(jnp.int32, (1, tk), 1)

        def fix(x, r0):
            if ck is not None:
                x = x - ck
            if masked:
                x = jnp.where(kpos < qlim[r0:r0 + STRIP, :], x, NEG)
            return x

        _softmax_tile(s, _pv(v_ref[0, pl.ds(ks, tk), :]), state,
                      fix if (masked or ck is not None) else None, slot=slot)

    def full_pair(jj, c):
        step(2 * jj, False, 0)
        step(2 * jj + 1, False, 1)
        return c

    def masked_step(j, c):
        step(j, True, 0)
        return c

    lax.fori_loop(0, n_full // 2, full_pair, 0)

    @pl.when(n_full % 2 == 1)
    def _():
        step(n_full - 1, False, 0)

    lax.fori_loop(n_full, n_need, masked_step, 0)

    o = _softmax_out(state)
    if mode == "diff":
        o = _diff_out(o, tq, lam_ref, gain_ref, li_ref)
    o_ref[0] = o.astype(BF16)


def _flash(mode, q, q_off, k, k_off, v, v_off, extra, *, tq, tk):
    b, t, _ = q.shape
    heads = C_HEADS if mode == "fox" else D_HEADS
    stack = 1 if mode == "fox" else 2
    chunk = 1 if mode == "fox" else CHUNK
    in_specs = [pl.BlockSpec((1, tq, LANE), lambda bi, h, i: (bi, i, q_off + h)),
                pl.BlockSpec((1, t, LANE), lambda bi, h, i: (bi, 0, k_off + h)),
                pl.BlockSpec((1, t, LANE), lambda bi, h, i: (bi, 0, v_off + h))]
    if mode == "fox":
        in_specs.append(pl.BlockSpec((1, 1, t // tk, 1, tk), lambda bi, h, i: (bi, h, 0, 0, 0)))
    else:
        in_specs += [pl.BlockSpec(e.shape, lambda bi, h, i: (0, 0)) for e in extra]
    body = functools.partial(_flash_body, tq=tq, tk=tk, chunk=chunk, mode=mode)
    return pl.pallas_call(
        body, grid=(b, heads, t // tq), in_specs=in_specs,
        out_specs=pl.BlockSpec((1, tq, LANE), lambda bi, h, i: (bi, i, h)),
        out_shape=jax.ShapeDtypeStruct((b, t, heads * LANE), BF16),
        scratch_shapes=_softmax_scratch(stack * tq, tk),
        compiler_params=_params(3), name=mode + "_prefill")(q, k, v, *extra)


def _decode_body(*refs, t, past_len, mode):
    if mode == "fox":
        q_ref, cache_ref, kn_ref, vn_ref, ck_ref = refs[:5]
        o_ref, state = refs[5], refs[6:]
    else:
        q_ref, cache_ref, kn_ref, vn_ref, lam_ref, gain_ref, li_ref = refs[:7]
        o_ref, state = refs[7], refs[8:]
    heads = C_HEADS
    stack = 1 if mode == "fox" else 2
    rows = stack * t
    tt = lax.broadcasted_iota(jnp.int32, (t, 1), 0)
    qlim = jnp.minimum(_row_limits(past_len + tt, 1 if mode == "fox" else CHUNK), past_len + t)
    qlim = jnp.concatenate([qlim] * stack, axis=0)
    new_pos = past_len + lax.broadcasted_iota(jnp.int32, (1, LANE), 1)
    zpad = jnp.zeros((LANE - t, LANE), BF16)
    _softmax_init(state[0], state[1], state[3])

    for h in range(heads):
        hs = slice(h * LANE, (h + 1) * LANE)
        base = h * rows
        qh = q_ref[0, :, hs]
        if mode == "fox":
            kt = cache_ref[0, 0, 0:HD, h, :].astype(BF16)
            vt = cache_ref[0, 0, HD:2 * HD, h, :].astype(BF16)
            ck_past = ck_ref[0, h, :, 0:past_len] * LOG2E
            ck_new = ck_ref[0, h, :, past_len:past_len + LANE] * LOG2E
            s = jnp.dot(qh, kt, preferred_element_type=F32)
            _softmax_tile(s, lambda p, vt=vt: _qk(p, vt), state,
                          lambda x, r0, ck_past=ck_past: x - ck_past, base)
        else:
            qh = _split_subheads(qh)
            k = cache_ref[0, 0, :, h, 0:2 * D_QK].astype(BF16)
            v = cache_ref[0, 0, :, h, 2 * D_QK:].astype(BF16)
            ck_new = None
            _softmax_tile(_qk(qh, k), _pv(v), state, None, base)

        def fix(x, r0, ck_new=ck_new):
            if ck_new is not None:
                x = x - ck_new
            return jnp.where(new_pos < qlim[r0:r0 + STRIP, :], x, NEG)

        kn = jnp.concatenate([kn_ref[0, :, hs], zpad], axis=0)
        vn = jnp.concatenate([vn_ref[0, :, hs], zpad], axis=0)
        _softmax_tile(_qk(qh, kn), _pv(vn), state, fix, base)

    o = _softmax_out(state)
    for h in range(heads):
        oh = o[h * rows:(h + 1) * rows, :]
        if mode == "diff":
            oh = _diff_out(oh, t, lam_ref, gain_ref, li_ref)
        o_ref[0, :, h * LANE:(h + 1) * LANE] = oh.astype(BF16)


def _decode(mode, layer, q, q_blk, cache, k_new, k_blk, v_new, v_blk, extra, *, past_len):
    b, t, _ = q.shape
    heads = C_HEADS
    assert t <= LANE and t % STRIP == 0 and past_len % LANE == 0
    stack = 1 if mode == "fox" else 2
    wide = lambda blk: pl.BlockSpec((1, t, heads * LANE), lambda bi: (bi, 0, blk))
    in_specs = [wide(q_blk), pl.BlockSpec((1, 1) + cache.shape[2:], lambda bi: (layer, bi, 0, 0, 0)),
                wide(k_blk), wide(v_blk)]
    if mode == "fox":
        in_specs.append(pl.BlockSpec((1,) + extra[0].shape[1:], lambda bi: (bi, 0, 0, 0)))
    else:
        in_specs += [pl.BlockSpec(e.shape, lambda bi: (0, 0)) for e in extra]
    body = functools.partial(_decode_body, t=t, past_len=past_len, mode=mode)
    return pl.pallas_call(
        body, grid=(b,), in_specs=in_specs, out_specs=wide(0),
        out_shape=jax.ShapeDtypeStruct((b, t, heads * LANE), BF16),
        scratch_shapes=_softmax_scratch(stack * heads * t, past_len),
        compiler_params=_params(1), name=mode + "_decode")(q, cache, k_new, v_new, *extra)


def _dsa_body(*refs, tq, tk, past_len, n_keys, k_sel, cached):
    if cached:
        (q_ref, iq_ref, iw_ref, cache_ref, knt_ref, vnt_ref, iknt_ref, o_ref,
         keys, bias_sc, kt_all, vt_all, ikt_all) = refs[:13]
        state = refs[13:]
        for dst, new, lo, hi in ((kt_all, knt_ref, 0, HD), (vt_all, vnt_ref, HD, 2 * HD),
                                 (ikt_all, iknt_ref, 2 * HD, 2 * HD + IDX_HD)):
            dst[:, 0:past_len] = cache_ref[0, 0, lo:hi, :].astype(BF16)
            dst[:, past_len:] = new[0]
        index_logits = lambda iq, ks: jnp.dot(iq[:, 0:IDX_HD], ikt_all[...], preferred_element_type=F32)
        key_logits = lambda q, ks: jnp.dot(q, kt_all[...], preferred_element_type=F32)
        values = lambda ks: (lambda p: _qk(p, vt_all[...]))
    else:
        q_ref, iq_ref, iw_ref, k_ref, v_ref, ik_ref, o_ref, keys, bias_sc = refs[:9]
        state = refs[9:]
        index_logits = lambda iq, ks: _qk(iq, ik_ref[0, pl.ds(ks, tk), :])
        key_logits = lambda q, ks: _qk(q, k_ref[0, pl.ds(ks, tk), :])
        values = lambda ks: _pv(v_ref[0, pl.ds(ks, tk), :])

    q0 = past_len + pl.program_id(1) * tq
    _, n_need = _tile_counts(q0, tq, tk, CHUNK, n_keys)
    qlim = jnp.minimum(_row_limits(q0 + lax.broadcasted_iota(jnp.int32, (tq, 1), 0), CHUNK), n_keys)

    iqf = iq_ref[0]
    parts = []
    for h in range(IDX_HEADS):
        blk = iqf[:, (h // 2) * LANE:(h // 2 + 1) * LANE]
        parts.append(pltpu.roll(blk, IDX_HD, 1) if h % 2 else blk)
    iq = jnp.concatenate(parts, axis=0).astype(BF16)
    iw = iw_ref[0]
    w_cols = [iw[:, IW_LANE + h:IW_LANE + h + 1] for h in range(IDX_HEADS)]

    def score_tile(j, c):
        ks = pl.multiple_of(j * tk, tk)
        rel = index_logits(iq, ks)
        sc = jnp.zeros((tq, tk), F32)
        for h in range(IDX_HEADS):
            sc = sc + jnp.maximum(rel[h * tq:(h + 1) * tq, :], 0.0) * w_cols[h]
        sc = sc * IDX_SCALE
        sc = jnp.where(sc == 0.0, 0.0, sc)
        bits = lax.bitcast_convert_type(sc, jnp.int32)
        key = jnp.where(bits < 0, bits ^ jnp.int32(0x7FFFFFFF), bits)
        kpos = ks + lax.broadcasted_iota(jnp.int32, (1, tk), 1)
        keys[j] = jnp.where(kpos < qlim, key, INT_MIN)
        return c

    lax.fori_loop(0, n_need, score_tile, 0)

    def count(pred):
        def body(j, acc):
            hit = jnp.where(pred(keys[j], j * tk), 1.0, 0.0)
            for c in range(tk // LANE):
                acc = acc + hit[:, c * LANE:(c + 1) * LANE]
            return acc
        acc = lax.fori_loop(0, n_need, body, jnp.zeros((tq, LANE), F32))
        return jnp.sum(acc, axis=1, keepdims=True)

    def count_ge(cand):
        return count(lambda t, _: t >= cand)

    def any_row(flag):
        return (jnp.max(jnp.where(flag, 1.0, 0.0)) > 0.0).astype(jnp.int32)

    want = jnp.float32(k_sel)
    crowded = count_ge(jnp.full((tq, 1), INT_MIN + 1, jnp.int32)) > want

    zero = jnp.zeros((tq, 1), jnp.int32)
    n_nonneg = count_ge(zero)
    at_zero = crowded & (count_ge(zero + 1) < want) & (n_nonneg >= want)

    def search_on(st):
        return (st[0] < 32) & (st[1] > 0)

    def value_bit(st):
        b, _, ans, n_ans = st
        cand = ans + lax.shift_left(jnp.int32(1), 31 - b)
        n = count_ge(cand)
        take = (n >= want) & jnp.logical_not(at_zero)
        ans = jnp.where(take, cand, ans)
        n_ans = jnp.where(take, n, n_ans)
        return b + 1, any_row(crowded & (n_ans != want)), ans, n_ans

    n_start = jnp.where(at_zero, n_nonneg, 2.0 * k_sel + 1.0)
    start = (jnp.int32(0), any_row(crowded & (n_start != want)),
             jnp.where(at_zero, jnp.int32(0), INT_MIN), n_start)
    _, _, kstar, n_ge = lax.while_loop(search_on, value_bit, start)
    tie = crowded & (n_ge > want)

    @pl.when(any_row(tie) > 0)
    def _():
        need = want - count_ge(kstar + 1)
        pw = next(w for w in (512, 256, LANE) if tk % w == 0)
        r = lax.broadcasted_iota(jnp.int32, (pw, pw), 0)
        c = lax.broadcasted_iota(jnp.int32, (pw, pw), 1)
        prefix = jnp.where(r <= c, 1.0, 0.0).astype(BF16)

        def drop(j, seen):
            t = keys[j]
            eq = tie & (t == kstar)
            kept = []
            for c0 in range(0, tk, pw):
                eq_c = eq[:, c0:c0 + pw]
                rank = seen + jnp.dot(jnp.where(eq_c, 1.0, 0.0).astype(BF16), prefix,
                                      preferred_element_type=F32)
                kept.append(jnp.where(eq_c & (rank > need), INT_MIN, t[:, c0:c0 + pw]))
                seen = rank[:, pw - 1:pw]
            keys[j] = kept[0] if len(kept) == 1 else jnp.concatenate(kept, axis=1)
            return seen

        lax.fori_loop(0, n_need, drop, jnp.zeros((tq, 1), F32))

    thr = jnp.maximum(kstar, INT_MIN + 1)
    qf = q_ref[0]
    q = jnp.concatenate([qf[:, h * LANE:(h + 1) * LANE] for h in range(B_HEADS)], axis=0)
    _softmax_init(state[0], state[1], state[3])

    def attend(j, slot):
        ks = pl.multiple_of(j * tk, tk)
        bias_sc[slot] = jnp.where(keys[j] >= thr, 0.0, NEG)
        _softmax_tile(key_logits(q, ks), values(ks), state,
                      lambda x, r0: x + bias_sc[slot, r0 % tq:r0 % tq + STRIP, :], slot=slot)

    def attend_pair(jj, c):
        attend(2 * jj, 0)
        attend(2 * jj + 1, 1)
        return c

    lax.fori_loop(0, n_need // 2, attend_pair, 0)

    @pl.when(n_need % 2 == 1)
    def _():
        attend(n_need - 1, 0)

    o = _softmax_out(state)
    for h in range(B_HEADS):
        o_ref[0, :, h * LANE:(h + 1) * LANE] = o[h * tq:(h + 1) * tq, :].astype(BF16)


def _dsa(q, iq, iw, kv, *, layer, past_len, n_keys, tq, tk):
    b, t, _ = q.shape
    cached = past_len > 0
    lpad = -(-n_keys // tk) * tk
    assert not cached or (lpad == tk and past_len % LANE == 0)
    k_sel = min(TOPK_MAX, n_keys // 4)
    body = functools.partial(_dsa_body, tq=tq, tk=tk, past_len=past_len, n_keys=n_keys, k_sel=k_sel,
                             cached=cached)
    rows = B_HEADS * tq
    in_specs = [pl.BlockSpec((1, tq, B_HEADS * LANE), lambda bi, i: (bi, i, R_BQ // B_HEADS)),
                pl.BlockSpec((1, tq, 2 * LANE), lambda bi, i: (bi, i, R_BIQ // 2)),
                pl.BlockSpec((1, tq, LANE), lambda bi, i: (bi, i, P_SMALL))]
    scratch = [pltpu.VMEM((lpad // tk, tq, tk), jnp.int32), pltpu.VMEM((2, tq, tk), F32)]
    if cached:
        cache = kv[0]
        in_specs.append(pl.BlockSpec((1, 1) + cache.shape[2:], lambda bi, i: (layer, bi, 0, 0)))
        in_specs += [pl.BlockSpec((1,) + a.shape[1:], lambda bi, i: (bi, 0, 0)) for a in kv[1:]]
        args = kv
        scratch += [pltpu.VMEM((a.shape[1], lpad), BF16) for a in kv[1:]]
    else:
        k, k_blk, v, v_blk, ik, ik_blk = kv
        in_specs += [pl.BlockSpec((1, n_keys, LANE), lambda bi, i: (bi, 0, k_blk)),
                     pl.BlockSpec((1, n_keys, LANE), lambda bi, i: (bi, 0, v_blk)),
                     pl.BlockSpec((1, n_keys, LANE), lambda bi, i: (bi, 0, ik_blk))]
        args = (k, v, ik)
    return pl.pallas_call(
        body, grid=(b, t // tq), in_specs=in_specs,
        out_specs=pl.BlockSpec((1, tq, B_HEADS * LANE), lambda bi, i: (bi, i, 0)),
        out_shape=jax.ShapeDtypeStruct((b, t, B_HEADS * LANE), BF16),
        scratch_shapes=scratch + _softmax_scratch(rows, tk),
        compiler_params=_params(2), name="dsa_decode" if cached else "dsa_prefill")(q, iq, iw, *args)


def _merge_body(ya_ref, yb_ref, yc_ref, yd_ref, g_ref, wb_ref, o_ref):
    d = o_ref.shape[1]
    acc = 0.0
    for m, y_ref in enumerate((ya_ref, yb_ref, yc_ref, yd_ref)):
        acc = acc + g_ref[:, m * d:(m + 1) * d].astype(F32) * jnp.dot(
            y_ref[...], wb_ref[m], preferred_element_type=F32)
    o_ref[...] = acc.astype(BF16)


def _merge(ys, gate, w_branch):
    m, d = gate.shape[0], w_branch.shape[2]
    tm = _tile(m, 256)
    y_spec = pl.BlockSpec((tm, BR_W), lambda i: (i, 0))
    return pl.pallas_call(
        _merge_body, grid=(m // tm,),
        in_specs=[y_spec] * N_BRANCH + [pl.BlockSpec((tm, N_BRANCH * d), lambda i: (i, 0)),
                                        pl.BlockSpec(w_branch.shape, lambda i: (0, 0, 0))],
        out_specs=pl.BlockSpec((tm, d), lambda i: (i, 0)),
        out_shape=jax.ShapeDtypeStruct((m, d), BF16),
        compiler_params=_params(1), name="merge")(*ys, gate, w_branch)


def _layernorm(z, g, b):
    mu = jnp.mean(z, axis=-1, keepdims=True)
    zc = z - mu
    var = jnp.mean(zc * zc, axis=-1, keepdims=True)
    return zc * lax.rsqrt(var + LN_EPS) * g + b


def _wo_ln_body(m_ref, x_ref, w_ref, g_ref, b_ref, of_ref, ob_ref, *, alpha):
    h = jnp.dot(m_ref[...], w_ref[...], preferred_element_type=F32)
    y = _layernorm(alpha * x_ref[...] + h, g_ref[...], b_ref[...])
    of_ref[...] = y
    ob_ref[...] = y.astype(BF16)


def _wo_ln(merged, x, w_o, g, b, alpha):
    m, d = x.shape
    tm = _tile(m, 256)
    row = pl.BlockSpec((tm, d), lambda i: (i, 0))
    vec = pl.BlockSpec((1, d), lambda i: (0, 0))
    return pl.pallas_call(
        functools.partial(_wo_ln_body, alpha=alpha), grid=(m // tm,),
        in_specs=[row, row, pl.BlockSpec((d, d), lambda i: (0, 0)), vec, vec],
        out_specs=[row, row],
        out_shape=[jax.ShapeDtypeStruct((m, d), F32), jax.ShapeDtypeStruct((m, d), BF16)],
        compiler_params=_params(1), name="wo_ln")(merged, x, w_o, g, b)


def _ffn_body(xb_ref, xf_ref, wu_ref, wd_ref, g_ref, b_ref, of_ref, ob_ref, acc, *, alpha):
    f = pl.program_id(1)

    @pl.when(f == 0)
    def _():
        acc[...] = jnp.zeros_like(acc)

    h = jnp.maximum(jnp.dot(xb_ref[...], wu_ref[...], preferred_element_type=F32), 0.0)
    acc[...] += jnp.dot((h * h).astype(BF16), wd_ref[...], preferred_element_type=F32)

    @pl.when(f == pl.num_programs(1) - 1)
    def _():
        y = _layernorm(alpha * xf_ref[...] + acc[...], g_ref[...], b_ref[...])
        of_ref[...] = y
        ob_ref[...] = y.astype(BF16)


def _ffn(xb, xf, w_up, w_down, g, b, alpha):
    m, d = xf.shape
    dff = w_up.shape[1]
    tm = _tile(m, 512)
    tf = _tile(dff, 512, LANE)
    row = pl.BlockSpec((tm, d), lambda i, f: (i, 0))
    vec = pl.BlockSpec((1, d), lambda i, f: (0, 0))
    return pl.pallas_call(
        functools.partial(_ffn_body, alpha=alpha), grid=(m // tm, dff // tf),
        in_specs=[row, row, pl.BlockSpec((d, tf), lambda i, f: (0, f)),
                  pl.BlockSpec((tf, d), lambda i, f: (f, 0)), vec, vec],
        out_specs=[row, row],
        out_shape=[jax.ShapeDtypeStruct((m, d), F32), jax.ShapeDtypeStruct((m, d), BF16)],
        scratch_shapes=[pltpu.VMEM((tm, d), F32)],
        compiler_params=_params(2), name="ffn")(xb, xf, w_up, w_down, g, b)


def _prep_weights(w_in, w_branch, w_o, w_up, w_down):
    d = w_in.shape[1]
    sizes = (BR_W, BR_W, BR_W, B_HEADS * HD, HD, HD, IDX_HEADS * IDX_HD, IDX_HD, IDX_HEADS,
             C_HEADS * HD, C_HEADS * HD, C_HEADS * HD, C_HEADS,
             D_HEADS * 2 * D_QK, D_HEADS * 2 * D_QK, D_HEADS * D_V, N_BRANCH * d)
    names = ("a_b", "a_c", "a_h", "b_q", "b_k", "b_v", "b_iq", "b_ik", "b_iw",
             "c_q", "c_k", "c_v", "c_f", "d_q", "d_k", "d_v", "gate")
    starts = np.concatenate([[0], np.cumsum(sizes)])
    col = {n: w_in[:, :, int(starts[i]):int(starts[i + 1])] for i, n in enumerate(names)}
    depth = w_in.shape[0]

    def zeros(n):
        return jnp.zeros((depth, d, n), w_in.dtype)

    w_a = jnp.concatenate([col["a_b"], col["a_c"], col["a_h"]], axis=2)
    w_r = jnp.concatenate([col["b_q"], col["b_iq"], col["b_k"], col["b_ik"], zeros(LANE - IDX_HD),
                           col["d_q"], col["d_k"]], axis=2)
    w_p = jnp.concatenate([col["c_q"], col["c_k"], col["c_v"], col["d_v"], col["b_v"],
                           col["b_iw"], col["c_f"], zeros(LANE - IDX_HEADS - C_HEADS)], axis=2)
    assert w_r.shape[2] == R_BLOCKS * LANE and w_p.shape[2] == P_BLOCKS * LANE
    cast = lambda w: w.astype(BF16)
    return dict(w_a=cast(w_a), w_r=cast(w_r), w_p=cast(w_p), w_g=cast(col["gate"]),
                w_branch=cast(w_branch), w_o=cast(w_o), w_up=cast(w_up), w_down=cast(w_down))


def _cols(a, block, n_blocks=1):
    return a[:, :, block * LANE:(block + n_blocks) * LANE]


def _trunk(x, past_len, conv_state, dsa_rows, fox_rows, diff_rows, wts, conv_w, fox_fbias,
           diff_lambda, diff_gain, ln1_g, ln1_b, ln2_g, ln2_b):
    b, t, d = x.shape
    depth = conv_w.shape[0]
    alpha = (2 * depth) ** 0.25
    m = b * t
    n_keys = past_len + t
    pos = jnp.tile(past_len + jnp.arange(t, dtype=jnp.int32), b)
    tab = _rope_tables(pos)
    if past_len:
        fox_cache = jnp.transpose(fox_rows, (0, 1, 4, 3, 2))
        dsa_cache = jnp.transpose(dsa_rows, (0, 1, 3, 2))
        tq_dsa, tk_dsa = t, -(-n_keys // LANE) * LANE
        new_pad = tk_dsa - past_len

        def new_t(a):
            return jnp.pad(jnp.transpose(a, (0, 2, 1)), ((0, 0), (0, 0), (0, new_pad - t))).astype(BF16)
    else:
        tq_att = _tile(t, 512)
        tk_att = _tile(t, 512, LANE)
        tq_dsa, tk_dsa = _tile(t, 128), tk_att

    xf = x.reshape(m, d)
    xb = xf.astype(BF16)
    conv_out, dsa_out, fox_out, diff_out = [], [], [], []
    for l in range(depth):
        lam_init = 0.8 - 0.6 * float(np.exp(-0.3 * l))
        u_a = _matmul("proj_conv", xb, wts["w_a"][l])[0].reshape(b, t, 3 * BR_W)
        rf, rb = _matmul("proj_rope", xb, wts["w_r"][l], tab=tab, kinds=R_KINDS, scales=R_SCALES,
                         want_b16=True)
        pf, pb = _matmul("proj_plain", xb, wts["w_p"][l], kinds=P_KINDS, scales=P_SCALES, want_b16=True)
        (gate,) = _matmul("proj_gate", xb, wts["w_g"][l], sigmoid=True, want_f32=False, want_b16=True,
                          tm_target=512)
        rf, rb, pf, pb = (a.reshape(b, t, a.shape[1]) for a in (rf, rb, pf, pb))

        y_a, n_conv = _short_conv(u_a, conv_state[l], conv_w[l])

        small = _cols(pf, P_SMALL)
        bias_row = jnp.zeros((1, LANE), F32).at[0, CF_LANE:CF_LANE + C_HEADS].set(fox_fbias[l])
        if past_len:
            past_lf = jnp.transpose(fox_cache[l, :, 2 * HD], (0, 2, 1))
            past_lf = jnp.pad(past_lf, ((0, 0), (0, 0), (CF_LANE, LANE - CF_LANE - C_HEADS)))
            small = jnp.concatenate([past_lf, small], axis=1)
        lf, cum = _fox_prep(small, bias_row, past_len)
        lf_new = lf[:, past_len:, CF_LANE:CF_LANE + C_HEADS]
        cum = cum[:, :, CF_LANE:CF_LANE + C_HEADS]
        diff_extra = (diff_lambda[l], diff_gain[l].reshape(1, D_V), jnp.full((1, 1), lam_init, F32))
        if past_len:
            ck = jnp.pad(jnp.transpose(cum, (0, 2, 1)), ((0, 0), (0, 0), (0, past_len + LANE - n_keys)))
            y_c = _decode("fox", l, pb, P_CQ // 4, fox_cache, pb, P_CK // 4, pb, P_CV // 4,
                          (ck[:, :, None, :],), past_len=past_len)
            y_d = _decode("diff", l, rb, R_DQ // 4, diff_rows, rb, R_DK // 4, pb, P_DV // 4,
                          diff_extra, past_len=past_len)
            dsa_kv = (dsa_cache, new_t(_cols(rf, R_BK)), new_t(_cols(pf, P_BV)),
                      new_t(_cols(rf, R_BIK)[..., :IDX_HD]))
        else:
            ck = jnp.transpose(cum, (0, 2, 1)).reshape(b, C_HEADS, t // tk_att, 1, tk_att)
            y_c = _flash("fox", pb, P_CQ, pb, P_CK, pb, P_CV, (ck,), tq=tq_att, tk=tk_att)
            y_d = _flash("diff", rb, R_DQ, rb, R_DK, pb, P_DV, diff_extra, tq=tq_att, tk=tk_att)
            dsa_kv = (rb, R_BK, pb, P_BV, rb, R_BIK)
        y_b = _dsa(rb, rf, pf, dsa_kv, layer=l, past_len=past_len, n_keys=n_keys, tq=tq_dsa, tk=tk_dsa)

        ys = [y.reshape(m, BR_W) for y in (y_a, y_b, y_c, y_d)]
        merged = _merge(ys, gate, wts["w_branch"][l])
        xf, xb = _wo_ln(merged, xf, wts["w_o"][l], ln1_g[l].reshape(1, d), ln1_b[l].reshape(1, d), alpha)
        xf, xb = _ffn(xb, xf, wts["w_up"][l], wts["w_down"][l], ln2_g[l].reshape(1, d),
                      ln2_b[l].reshape(1, d), alpha)

        conv_out.append(n_conv)
        dsa_out.append(jnp.concatenate([_cols(rf, R_BK), _cols(pf, P_BV), _cols(rf, R_BIK)[..., :IDX_HD]],
                                       axis=-1))
        fox_out.append(jnp.concatenate([_cols(pf, P_CK, 4).reshape(b, t, C_HEADS, HD),
                                        _cols(pf, P_CV, 4).reshape(b, t, C_HEADS, HD),
                                        lf_new[..., None]], axis=-1))
        diff_out.append(jnp.concatenate([_cols(rf, R_DK, 4).reshape(b, t, D_HEADS, 2 * D_QK),
                                         _cols(pf, P_DV, 4).reshape(b, t, D_HEADS, D_V)], axis=-1))
    return (xf.reshape(b, t, d), jnp.stack(conv_out), jnp.stack(dsa_out), jnp.stack(fox_out),
            jnp.stack(diff_out))


def kernel(x_prompt, x_sample, state_conv, cache_dsa, cache_fox, cache_diff, w_in, conv_w, fox_fbias,
           diff_lambda, diff_gain, w_branch, w_o, ln1_g, ln1_b, w_up, w_down, ln2_g, ln2_b):
    depth = w_in.shape[0]
    b = x_prompt.shape[0]
    wts = _prep_weights(w_in, w_branch, w_o, w_up, w_down)
    shared = (wts, conv_w, fox_fbias, diff_lambda, diff_gain, ln1_g, ln1_b, ln2_g, ln2_b)
    conv0 = jnp.zeros((depth, b, CONV_W - 1, BR_W), x_prompt.dtype)
    y_p, conv_p, dsa_p, fox_p, diff_p = _trunk(x_prompt, 0, conv0, None, None, None, *shared)
    y_s, conv_s, dsa_s, fox_s, diff_s = _trunk(x_sample, cache_dsa.shape[2], state_conv, cache_dsa,
                                               cache_fox, cache_diff, *shared)
    return (y_p, y_s, conv_p, dsa_p, fox_p, diff_p, conv_s, dsa_s, fox_s, diff_s)
```

```python
import functools

import numpy as np
import jax
import jax.numpy as jnp
from jax import lax
from jax.experimental import pallas as pl
from jax.experimental.pallas import tpu as pltpu

CHUNK = 64
N_BRANCH = 4
BR_W = 512
HD = 128
CONV_W = 3
B_HEADS = 4
IDX_HEADS = 4
IDX_HD = 64
TOPK_MAX = 256
IDX_SCALE = (IDX_HEADS * IDX_HD) ** -0.5
C_HEADS = 4
D_HEADS = 4
D_QK = 64
D_V = 128
ROPE_THETA = 500000.0
LN_EPS = 1e-5
RMS_EPS = 1e-5

LANE = 128
STRIP = 16
VMEM_LIMIT = 56 * 1024 * 1024
NEG = -1e30
LOG2E = 1.4426950408889634
INT_MIN = np.int32(-2 ** 31)
F32 = jnp.float32
BF16 = jnp.bfloat16

R_BQ, R_BIQ, R_BK, R_BIK, R_DQ, R_DK, R_BLOCKS = 0, 4, 6, 7, 8, 12, 16
P_CQ, P_CK, P_CV, P_DV, P_BV, P_SMALL, P_BLOCKS = 0, 4, 8, 12, 16, 17, 18
IW_LANE, CF_LANE = 0, IDX_HEADS
R_KINDS = ("r128",) * 4 + ("r64",) * 2 + ("r128", "r64") + ("r64",) * 8
R_SCALES = (HD ** -0.5 * LOG2E,) * 4 + (1.0,) * 4 + (D_QK ** -0.5 * LOG2E,) * 4 + (1.0,) * 4
P_KINDS = (None,) * P_BLOCKS
P_SCALES = (HD ** -0.5 * LOG2E,) * 4 + (1.0,) * 14


def _tile(n, target, align=8):
    if n <= target:
        return n
    for t in range(target - target % align, 0, -align):
        if n % t == 0:
            return t
    return n


def _params(n_grid):
    return pltpu.CompilerParams(dimension_semantics=("arbitrary",) * n_grid,
                                vmem_limit_bytes=VMEM_LIMIT)


def _mm_body(x_ref, w_ref, *rest, kinds, scales, has_tab, sigmoid, want_f32, want_b16):
    tab_ref = rest[0] if has_tab else None
    outs = rest[1:] if has_tab else rest
    of_ref = outs[0] if want_f32 else None
    ob_ref = outs[-1] if want_b16 else None
    acc = jnp.dot(x_ref[...], w_ref[...], preferred_element_type=F32)
    if kinds is None:
        if sigmoid:
            acc = jax.nn.sigmoid(acc)
        if want_f32:
            of_ref[...] = acc
        if want_b16:
            ob_ref[...] = acc.astype(BF16)
        return
    for c, (kind, scale) in enumerate(zip(kinds, scales)):
        sl = slice(c * LANE, (c + 1) * LANE)
        blk = acc[:, sl]
        if kind is not None:
            base = 0 if kind == "r128" else 3 * LANE
            half = (HD // 4) // 2 if kind == "r128" else (IDX_HD // 4) // 2
            cos = tab_ref[:, base:base + LANE]
            s_lo = tab_ref[:, base + LANE:base + 2 * LANE]
            s_hi = tab_ref[:, base + 2 * LANE:base + 3 * LANE]
            blk = (blk * cos + pltpu.roll(blk, LANE - half, 1) * s_lo
                   + pltpu.roll(blk, half, 1) * s_hi)
        if want_f32:
            of_ref[:, sl] = blk
        if want_b16:
            ob_ref[:, sl] = (blk * scale if scale != 1.0 else blk).astype(BF16)


def _matmul(name, x, w, *, tab=None, kinds=None, scales=None, sigmoid=False,
            want_f32=True, want_b16=False, tm_target=256, tn_target=2048):
    m, k = x.shape
    n = w.shape[1]
    tm = _tile(m, tm_target)
    tn = n if kinds is not None else _tile(n, tn_target, LANE)
    grid = (n // tn, m // tm)
    in_specs = [pl.BlockSpec((tm, k), lambda j, i: (i, 0)),
                pl.BlockSpec((k, tn), lambda j, i: (0, j))]
    args = [x, w]
    if tab is not None:
        in_specs.append(pl.BlockSpec((tm, tab.shape[1]), lambda j, i: (i, 0)))
        args.append(tab)
    out_shape, out_specs = [], []
    if want_f32:
        out_shape.append(jax.ShapeDtypeStruct((m, n), F32))
        out_specs.append(pl.BlockSpec((tm, tn), lambda j, i: (i, j)))
    if want_b16:
        out_shape.append(jax.ShapeDtypeStruct((m, n), BF16))
        out_specs.append(pl.BlockSpec((tm, tn), lambda j, i: (i, j)))
    body = functools.partial(_mm_body, kinds=kinds, scales=scales, has_tab=tab is not None,
                             sigmoid=sigmoid, want_f32=want_f32, want_b16=want_b16)
    return pl.pallas_call(body, grid=grid, in_specs=in_specs, out_specs=out_specs,
                          out_shape=out_shape, compiler_params=_params(2), name=name)(*args)


def _rope_tables(pos):
    tabs = []
    for hd in (HD, IDX_HD):
        rot = hd // 4
        half = rot // 2
        freqs = ROPE_THETA ** (-2.0 * jnp.arange(half, dtype=F32) / rot)
        ang = pos.astype(F32)[:, None] * freqs[None, :]
        cos, sin = jnp.cos(ang), jnp.sin(ang)
        m = pos.shape[0]
        one = jnp.ones((m, hd - rot), F32)
        zero_h = jnp.zeros((m, half), F32)
        zero_t = jnp.zeros((m, hd - rot), F32)
        c = jnp.concatenate([cos, cos, one], axis=1)
        s_lo = jnp.concatenate([-sin, zero_h, zero_t], axis=1)
        s_hi = jnp.concatenate([zero_h, sin, zero_t], axis=1)
        rep = LANE // hd
        tabs += [jnp.tile(c, (1, rep)), jnp.tile(s_lo, (1, rep)), jnp.tile(s_hi, (1, rep))]
    return jnp.concatenate(tabs, axis=1)


def _conv_body(u_ref, st_ref, w_ref, y_ref, ns_ref, carry, *, tm):
    @pl.when(pl.program_id(1) == 0)
    def _():
        carry[...] = st_ref[0]

    gate_b = u_ref[0, :, 0:BR_W]
    v = u_ref[0, :, BR_W:2 * BR_W] * u_ref[0, :, 2 * BR_W:3 * BR_W]
    p0 = carry[0:1, :]
    p1 = carry[1:2, :]
    row = lax.broadcasted_iota(jnp.int32, (tm, BR_W), 0)
    v1 = jnp.where(row == 0, p1, pltpu.roll(v, 1, 0))
    v2 = jnp.where(row == 0, p0, jnp.where(row == 1, p1, pltpu.roll(v, 2, 0)))
    w = w_ref[...]
    y = gate_b * (v2 * w[0:1, :] + v1 * w[1:2, :] + v * w[2:3, :])
    y_ref[0] = y.astype(BF16)
    tail = (u_ref[0, tm - 2:tm, BR_W:2 * BR_W] * u_ref[0, tm - 2:tm, 2 * BR_W:3 * BR_W])
    carry[...] = tail
    ns_ref[0] = tail


def _short_conv(u_a, state, conv_w):
    b, t, _ = u_a.shape
    assert t >= CONV_W - 1
    tm = _tile(t, 512)
    body = functools.partial(_conv_body, tm=tm)
    return pl.pallas_call(
        body, grid=(b, t // tm),
        in_specs=[pl.BlockSpec((1, tm, 3 * BR_W), lambda bi, i: (bi, i, 0)),
                  pl.BlockSpec((1, CONV_W - 1, BR_W), lambda bi, i: (bi, 0, 0)),
                  pl.BlockSpec((CONV_W, BR_W), lambda bi, i: (0, 0))],
        out_specs=[pl.BlockSpec((1, tm, BR_W), lambda bi, i: (bi, i, 0)),
                   pl.BlockSpec((1, CONV_W - 1, BR_W), lambda bi, i: (bi, 0, 0))],
        out_shape=[jax.ShapeDtypeStruct((b, t, BR_W), BF16),
                   jax.ShapeDtypeStruct((b, CONV_W - 1, BR_W), F32)],
        scratch_shapes=[pltpu.VMEM((CONV_W - 1, BR_W), F32)],
        compiler_params=_params(2), name="short_conv")(u_a, state, conv_w)


def _fox_prep_body(x_ref, bias_ref, lf_ref, cum_ref, carry, *, tl, past_len):
    i = pl.program_id(1)

    @pl.when(i == 0)
    def _():
        carry[...] = jnp.zeros_like(carry)

    x = x_ref[0]
    z = x + bias_ref[...]
    log_sig = jnp.minimum(z, 0.0) - jnp.log1p(jnp.exp(-jnp.abs(z)))
    row = i * tl + lax.broadcasted_iota(jnp.int32, (tl, 1), 0)
    lf = jnp.where(row >= past_len, log_sig, x)
    r = lax.broadcasted_iota(jnp.int32, (tl, tl), 0)
    c = lax.broadcasted_iota(jnp.int32, (tl, tl), 1)
    tri = jnp.where(c <= r, 1.0, 0.0).astype(F32)
    cs = jnp.dot(tri, lf, precision=lax.Precision.HIGHEST, preferred_element_type=F32) + carry[...]
    lf_ref[0] = lf
    cum_ref[0] = cs
    carry[...] = cs[tl - 1:tl, :]


def _fox_prep(x, bias_row, past_len):
    b, l, _ = x.shape
    tl = _tile(l, 512)
    body = functools.partial(_fox_prep_body, tl=tl, past_len=past_len)
    spec = pl.BlockSpec((1, tl, LANE), lambda bi, i: (bi, i, 0))
    return pl.pallas_call(
        body, grid=(b, l // tl),
        in_specs=[spec, pl.BlockSpec((1, LANE), lambda bi, i: (0, 0))],
        out_specs=[spec, spec],
        out_shape=[jax.ShapeDtypeStruct((b, l, LANE), F32)] * 2,
        scratch_shapes=[pltpu.VMEM((1, LANE), F32)],
        compiler_params=_params(2), name="fox_prep")(x, bias_row)


def _row_limits(qpos, chunk):
    if chunk == 1:
        return qpos + 1
    sh = chunk.bit_length() - 1
    return lax.shift_left(lax.shift_right_arithmetic(qpos, sh) + 1, sh)


def _tile_counts(q0, tq, tk, chunk, n_keys):
    lim_min = (q0 // chunk + 1) * chunk
    lim_max = ((q0 + tq - 1) // chunk + 1) * chunk
    n_full = jnp.minimum(lim_min, n_keys) // tk
    n_need = (jnp.minimum(lim_max, n_keys) + tk - 1) // tk
    return n_full, n_need


def _softmax_scratch(rows, tk):
    return [pltpu.VMEM((rows, LANE), F32)] * 4 + [pltpu.VMEM((2, rows, tk), BF16)]


def _softmax_init(m_sc, l_sc, acc_sc):
    m_sc[...] = jnp.full(m_sc.shape, NEG, F32)
    l_sc[...] = jnp.zeros(l_sc.shape, F32)
    acc_sc[...] = jnp.zeros(acc_sc.shape, F32)


def _softmax_tile(s, pv, state, fix=None, base=0, slot=0):
    m_sc, l_sc, a_sc, acc_sc, p_sc = state
    rows, tk = s.shape
    for r0 in range(0, rows, STRIP):
        rs = slice(base + r0, base + r0 + STRIP)
        x = s[r0:r0 + STRIP, :]
        if fix is not None:
            x = fix(x, r0)
        blocks = [x[:, c * LANE:(c + 1) * LANE] for c in range(tk // LANE)]
        mx = blocks[0]
        for blk in blocks[1:]:
            mx = jnp.maximum(mx, blk)
        m_old = m_sc[rs, :]
        m_new = jnp.maximum(m_old, jnp.max(mx, axis=1, keepdims=True))
        alpha = jnp.exp2(m_old - m_new)
        tot = None
        for c, blk in enumerate(blocks):
            p = jnp.exp2(blk - m_new)
            tot = p if tot is None else tot + p
            p_sc[slot, rs, c * LANE:(c + 1) * LANE] = p.astype(BF16)
        l_sc[rs, :] = alpha * l_sc[rs, :] + tot
        m_sc[rs, :] = m_new
        a_sc[rs, :] = alpha
    rs = slice(base, base + rows)
    acc_sc[rs, :] = a_sc[rs, :] * acc_sc[rs, :] + pv(p_sc[slot, rs, 0:tk])


def _pv(v):
    return lambda p: jnp.dot(p, v, preferred_element_type=F32)


def _softmax_out(state):
    _, l_sc, _, acc_sc, _ = state
    return acc_sc[...] / jnp.sum(l_sc[...], axis=1, keepdims=True)


def _qk(q, k):
    return lax.dot_general(q, k, (((1,), (1,)), ((), ())), preferred_element_type=F32)


def _split_subheads(q):
    qf = q.astype(F32)
    lane = lax.broadcasted_iota(jnp.int32, qf.shape, 1)
    return jnp.concatenate([jnp.where(lane < D_QK, qf, 0.0), jnp.where(lane >= D_QK, qf, 0.0)],
                           axis=0).astype(BF16)


def _diff_out(o, tq, lam_ref, gain_ref, li_ref):
    lam = lam_ref[...]
    li = li_ref[...]
    lmb = (jnp.exp(jnp.sum(lam[0:1, :] * lam[1:2, :], axis=1, keepdims=True))
           - jnp.exp(jnp.sum(lam[2:3, :] * lam[3:4, :], axis=1, keepdims=True)) + li)
    o = o[0:tq, :] - lmb * o[tq:2 * tq, :]
    o = o * lax.rsqrt(jnp.mean(o * o, axis=1, keepdims=True) + RMS_EPS)
    return o * gain_ref[...] * (1.0 - li)


def _flash_body(*refs, tq, tk, chunk, mode):
    if mode == "fox":
        q_ref, k_ref, v_ref, ck_ref = refs[:4]
        o_ref, state = refs[4], refs[5:]
    else:
        q_ref, k_ref, v_ref, lam_ref, gain_ref, li_ref = refs[:6]
        o_ref, state = refs[6], refs[7:]
    n_keys = k_ref.shape[1]
    q0 = pl.program_id(2) * tq
    n_full, n_need = _tile_counts(q0, tq, tk, chunk, n_keys)
    qlim = _row_limits(q0 + lax.broadcasted_iota(jnp.int32, (tq, 1), 0), chunk)
    if mode == "fox":
        q = q_ref[0]
    else:
        q = _split_subheads(q_ref[0])
        qlim = jnp.concatenate([qlim, qlim], axis=0)
    _softmax_init(state[0], state[1], state[3])

    def step(j, masked, slot):
        ks = pl.multiple_of(j * tk, tk)
        s = _qk(q, k_ref[0, pl.ds(ks, tk), :])
        ck = ck_ref[0, 0, j] * LOG2E if mode == "fox" else None
        kpos = ks + lax.broadcasted_iota(jnp.int32, (1, tk), 1)

        def fix(x, r0):
            if ck is not None:
                x = x - ck
            if masked:
                x = jnp.where(kpos < qlim[r0:r0 + STRIP, :], x, NEG)
            return x

        _softmax_tile(s, _pv(v_ref[0, pl.ds(ks, tk), :]), state,
                      fix if (masked or ck is not None) else None, slot=slot)

    def full_pair(jj, c):
        step(2 * jj, False, 0)
        step(2 * jj + 1, False, 1)
        return c

    def masked_step(j, c):
        step(j, True, 0)
        return c

    lax.fori_loop(0, n_full // 2, full_pair, 0)

    @pl.when(n_full % 2 == 1)
    def _():
        step(n_full - 1, False, 0)

    lax.fori_loop(n_full, n_need, masked_step, 0)

    o = _softmax_out(state)
    if mode == "diff":
        o = _diff_out(o, tq, lam_ref, gain_ref, li_ref)
    o_ref[0] = o.astype(BF16)


def _flash(mode, q, q_off, k, k_off, v, v_off, extra, *, tq, tk):
    b, t, _ = q.shape
    heads = C_HEADS if mode == "fox" else D_HEADS
    stack = 1 if mode == "fox" else 2
    chunk = 1 if mode == "fox" else CHUNK
    in_specs = [pl.BlockSpec((1, tq, LANE), lambda bi, h, i: (bi, i, q_off + h)),
                pl.BlockSpec((1, t, LANE), lambda bi, h, i: (bi, 0, k_off + h)),
                pl.BlockSpec((1, t, LANE), lambda bi, h, i: (bi, 0, v_off + h))]
    if mode == "fox":
        in_specs.append(pl.BlockSpec((1, 1, t // tk, 1, tk), lambda bi, h, i: (bi, h, 0, 0, 0)))
    else:
        in_specs += [pl.BlockSpec(e.shape, lambda bi, h, i: (0, 0)) for e in extra]
    body = functools.partial(_flash_body, tq=tq, tk=tk, chunk=chunk, mode=mode)
    return pl.pallas_call(
        body, grid=(b, heads, t // tq), in_specs=in_specs,
        out_specs=pl.BlockSpec((1, tq, LANE), lambda bi, h, i: (bi, i, h)),
        out_shape=jax.ShapeDtypeStruct((b, t, heads * LANE), BF16),
        scratch_shapes=_softmax_scratch(stack * tq, tk),
        compiler_params=_params(3), name=mode + "_prefill")(q, k, v, *extra)


def _decode_body(*refs, t, past_len, mode):
    if mode == "fox":
        q_ref, cache_ref, kn_ref, vn_ref, ck_ref = refs[:5]
        o_ref, state = refs[5], refs[6:]
    else:
        q_ref, cache_ref, kn_ref, vn_ref, lam_ref, gain_ref, li_ref = refs[:7]
        o_ref, state = refs[7], refs[8:]
    heads = C_HEADS
    stack = 1 if mode == "fox" else 2
    rows = stack * t
    tt = lax.broadcasted_iota(jnp.int32, (t, 1), 0)
    qlim = jnp.minimum(_row_limits(past_len + tt, 1 if mode == "fox" else CHUNK), past_len + t)
    qlim = jnp.concatenate([qlim] * stack, axis=0)
    new_pos = past_len + lax.broadcasted_iota(jnp.int32, (1, LANE), 1)
    zpad = jnp.zeros((LANE - t, LANE), BF16)
    _softmax_init(state[0], state[1], state[3])

    for h in range(heads):
        hs = slice(h * LANE, (h + 1) * LANE)
        base = h * rows
        qh = q_ref[0, :, hs]
        if mode == "fox":
            kt = cache_ref[0, 0, 0:HD, h, :].astype(BF16)
            vt = cache_ref[0, 0, HD:2 * HD, h, :].astype(BF16)
            ck_past = ck_ref[0, h, :, 0:past_len] * LOG2E
            ck_new = ck_ref[0, h, :, past_len:past_len + LANE] * LOG2E
            s = jnp.dot(qh, kt, preferred_element_type=F32)
            _softmax_tile(s, lambda p, vt=vt: _qk(p, vt), state,
                          lambda x, r0, ck_past=ck_past: x - ck_past, base)
        else:
            qh = _split_subheads(qh)
            k = cache_ref[0, 0, :, h, 0:2 * D_QK].astype(BF16)
            v = cache_ref[0, 0, :, h, 2 * D_QK:].astype(BF16)
            ck_new = None
            _softmax_tile(_qk(qh, k), _pv(v), state, None, base)

        def fix(x, r0, ck_new=ck_new):
            if ck_new is not None:
                x = x - ck_new
            return jnp.where(new_pos < qlim[r0:r0 + STRIP, :], x, NEG)

        kn = jnp.concatenate([kn_ref[0, :, hs], zpad], axis=0)
        vn = jnp.concatenate([vn_ref[0, :, hs], zpad], axis=0)
        _softmax_tile(_qk(qh, kn), _pv(vn), state, fix, base)

    o = _softmax_out(state)
    for h in range(heads):
        oh = o[h * rows:(h + 1) * rows, :]
        if mode == "diff":
            oh = _diff_out(oh, t, lam_ref, gain_ref, li_ref)
        o_ref[0, :, h * LANE:(h + 1) * LANE] = oh.astype(BF16)


def _decode(mode, layer, q, q_blk, cache, k_new, k_blk, v_new, v_blk, extra, *, past_len):
    b, t, _ = q.shape
    heads = C_HEADS
    assert t <= LANE and t % STRIP == 0 and past_len % LANE == 0
    stack = 1 if mode == "fox" else 2
    wide = lambda blk: pl.BlockSpec((1, t, heads * LANE), lambda bi: (bi, 0, blk))
    in_specs = [wide(q_blk), pl.BlockSpec((1, 1) + cache.shape[2:], lambda bi: (layer, bi, 0, 0, 0)),
                wide(k_blk), wide(v_blk)]
    if mode == "fox":
        in_specs.append(pl.BlockSpec((1,) + extra[0].shape[1:], lambda bi: (bi, 0, 0, 0)))
    else:
        in_specs += [pl.BlockSpec(e.shape, lambda bi: (0, 0)) for e in extra]
    body = functools.partial(_decode_body, t=t, past_len=past_len, mode=mode)
    return pl.pallas_call(
        body, grid=(b,), in_specs=in_specs, out_specs=wide(0),
        out_shape=jax.ShapeDtypeStruct((b, t, heads * LANE), BF16),
        scratch_shapes=_softmax_scratch(stack * heads * t, past_len),
        compiler_params=_params(1), name=mode + "_decode")(q, cache, k_new, v_new, *extra)


def _dsa_body(*refs, tq, tk, past_len, n_keys, k_sel, cached):
    if cached:
        (q_ref, iq_ref, iw_ref, cache_ref, knt_ref, vnt_ref, iknt_ref, o_ref,
         keys, keys_hi, bias_sc, kt_all, vt_all, ikt_all) = refs[:14]
        state = refs[14:]
        for dst, new, lo, hi in ((kt_all, knt_ref, 0, HD), (vt_all, vnt_ref, HD, 2 * HD),
                                 (ikt_all, iknt_ref, 2 * HD, 2 * HD + IDX_HD)):
            dst[:, 0:past_len] = cache_ref[0, 0, lo:hi, :].astype(BF16)
            dst[:, past_len:] = new[0]
        index_logits = lambda iq, ks: jnp.dot(iq[:, 0:IDX_HD], ikt_all[...], preferred_element_type=F32)
        key_logits = lambda q, ks: jnp.dot(q, kt_all[...], preferred_element_type=F32)
        values = lambda ks: (lambda p: _qk(p, vt_all[...]))
    else:
        q_ref, iq_ref, iw_ref, k_ref, v_ref, ik_ref, o_ref, keys, keys_hi, bias_sc = refs[:10]
        state = refs[10:]
        index_logits = lambda iq, ks: _qk(iq, ik_ref[0, pl.ds(ks, tk), :])
        key_logits = lambda q, ks: _qk(q, k_ref[0, pl.ds(ks, tk), :])
        values = lambda ks: _pv(v_ref[0, pl.ds(ks, tk), :])

    q0 = past_len + pl.program_id(1) * tq
    _, n_need = _tile_counts(q0, tq, tk, CHUNK, n_keys)
    qlim = jnp.minimum(_row_limits(q0 + lax.broadcasted_iota(jnp.int32, (tq, 1), 0), CHUNK), n_keys)

    iqf = iq_ref[0]
    parts = []
    for h in range(IDX_HEADS):
        blk = iqf[:, (h // 2) * LANE:(h // 2 + 1) * LANE]
        parts.append(pltpu.roll(blk, IDX_HD, 1) if h % 2 else blk)
    iq = jnp.concatenate(parts, axis=0).astype(BF16)
    iw = iw_ref[0]
    w_cols = [iw[:, IW_LANE + h:IW_LANE + h + 1] for h in range(IDX_HEADS)]

    def score_tile(j, c):
        ks = pl.multiple_of(j * tk, tk)
        rel = index_logits(iq, ks)
        sc = jnp.zeros((tq, tk), F32)
        for h in range(IDX_HEADS):
            sc = sc + jnp.maximum(rel[h * tq:(h + 1) * tq, :], 0.0) * w_cols[h]
        sc = sc * IDX_SCALE
        sc = jnp.where(sc == 0.0, 0.0, sc)
        bits = lax.bitcast_convert_type(sc, jnp.int32)
        key = jnp.where(bits < 0, bits ^ jnp.int32(0x7FFFFFFF), bits)
        kpos = ks + lax.broadcasted_iota(jnp.int32, (1, tk), 1)
        key = jnp.where(kpos < qlim, key, INT_MIN)
        keys[j] = key
        keys_hi[j] = lax.shift_right_arithmetic(key, 16).astype(jnp.int16)
        return c

    lax.fori_loop(0, n_need, score_tile, 0)

    def count(pred):
        def add(j, acc):
            hit = jnp.where(pred(keys[j]), 1.0, 0.0)
            for c in range(tk // LANE):
                acc = acc + hit[:, c * LANE:(c + 1) * LANE]
            return acc

        def pair(jj, acc):
            return add(2 * jj + 1, add(2 * jj, acc))

        acc = lax.fori_loop(0, n_need // 2, pair, jnp.zeros((tq, LANE), F32))
        acc = lax.cond(n_need % 2 == 1, lambda a: add(n_need - 1, a), lambda a: a, acc)
        return jnp.sum(acc, axis=1, keepdims=True)

    def count_ge(cand):
        return count(lambda t: t >= cand)

    def count_hi_ge(cand):
        cand = jnp.broadcast_to(cand, (tq, LANE)).astype(jnp.int16)
        one = jnp.ones((tq, LANE), BF16)

        def add(j, acc):
            t = keys_hi[j]
            for c in range(tk // LANE):
                acc = acc + jnp.where(t[:, c * LANE:(c + 1) * LANE] >= cand, one, jnp.zeros_like(one))
            return acc

        acc = lax.fori_loop(0, n_need, add, jnp.zeros((tq, LANE), BF16))
        return jnp.sum(acc.astype(F32), axis=1, keepdims=True)

    def any_row(flag):
        return (jnp.max(jnp.where(flag, 1.0, 0.0)) > 0.0).astype(jnp.int32)

    want = jnp.float32(k_sel)
    crowded = count_ge(jnp.full((tq, 1), INT_MIN + 1, jnp.int32)) > want

    zero = jnp.zeros((tq, 1), jnp.int32)
    n_nonneg = count_ge(zero)
    at_zero = crowded & (count_ge(zero + 1) < want) & (n_nonneg >= want)
    searching = crowded & jnp.logical_not(at_zero)

    def search_on(st):
        return (st[0] < 16) & (st[1] > 0)

    def search_bit(counter):
        def step(st):
            b, _, ans, n_ans = st
            cand = ans + lax.shift_left(jnp.int32(1), 15 - b)
            n = counter(cand)
            take = (n >= want) & searching
            ans = jnp.where(take, cand, ans)
            n_ans = jnp.where(take, n, n_ans)
            return b + 1, any_row(searching & (n_ans != want)), ans, n_ans
        return step

    start = (jnp.int32(0), any_row(searching), jnp.full((tq, 1), -2 ** 15, jnp.int32),
             jnp.full((tq, 1), 2.0 * k_sel + 1.0, F32))
    _, go, top, n_top = lax.while_loop(search_on, search_bit(count_hi_ge), start)
    start = (jnp.int32(0), go, jnp.where(at_zero, jnp.int32(0), lax.shift_left(top, 16)),
             jnp.where(at_zero, n_nonneg, n_top))
    _, _, kstar, n_ge = lax.while_loop(search_on, search_bit(count_ge), start)
    tie = crowded & (n_ge > want)

    @pl.when(any_row(tie) > 0)
    def _():
        need = want - count_ge(kstar + 1)
        pw = next(w for w in (512, 256, LANE) if tk % w == 0)
        r = lax.broadcasted_iota(jnp.int32, (pw, pw), 0)
        c = lax.broadcasted_iota(jnp.int32, (pw, pw), 1)
        prefix = jnp.where(r <= c, 1.0, 0.0).astype(BF16)

        def drop(j, seen):
            t = keys[j]
            eq = tie & (t == kstar)
            kept = []
            for c0 in range(0, tk, pw):
                eq_c = eq[:, c0:c0 + pw]
                rank = seen + jnp.dot(jnp.where(eq_c, 1.0, 0.0).astype(BF16), prefix,
                                      preferred_element_type=F32)
                kept.append(jnp.where(eq_c & (rank > need), INT_MIN, t[:, c0:c0 + pw]))
                seen = rank[:, pw - 1:pw]
            keys[j] = kept[0] if len(kept) == 1 else jnp.concatenate(kept, axis=1)
            return seen

        lax.fori_loop(0, n_need, drop, jnp.zeros((tq, 1), F32))

    thr = jnp.maximum(kstar, INT_MIN + 1)
    qf = q_ref[0]
    q = jnp.concatenate([qf[:, h * LANE:(h + 1) * LANE] for h in range(B_HEADS)], axis=0)
    _softmax_init(state[0], state[1], state[3])

    def attend(j, slot):
        ks = pl.multiple_of(j * tk, tk)
        bias_sc[slot] = jnp.where(keys[j] >= thr, 0.0, NEG)
        _softmax_tile(key_logits(q, ks), values(ks), state,
                      lambda x, r0: x + bias_sc[slot, r0 % tq:r0 % tq + STRIP, :], slot=slot)

    def attend_pair(jj, c):
        attend(2 * jj, 0)
        attend(2 * jj + 1, 1)
        return c

    lax.fori_loop(0, n_need // 2, attend_pair, 0)

    @pl.when(n_need % 2 == 1)
    def _():
        attend(n_need - 1, 0)

    o = _softmax_out(state)
    for h in range(B_HEADS):
        o_ref[0, :, h * LANE:(h + 1) * LANE] = o[h * tq:(h + 1) * tq, :].astype(BF16)


def _dsa(q, iq, iw, kv, *, layer, past_len, n_keys, tq, tk):
    b, t, _ = q.shape
    cached = past_len > 0
    lpad = -(-n_keys // tk) * tk
    assert not cached or (lpad == tk and past_len % LANE == 0)
    k_sel = min(TOPK_MAX, n_keys // 4)
    body = functools.partial(_dsa_body, tq=tq, tk=tk, past_len=past_len, n_keys=n_keys, k_sel=k_sel,
                             cached=cached)
    rows = B_HEADS * tq
    in_specs = [pl.BlockSpec((1, tq, B_HEADS * LANE), lambda bi, i: (bi, i, R_BQ // B_HEADS)),
                pl.BlockSpec((1, tq, 2 * LANE), lambda bi, i: (bi, i, R_BIQ // 2)),
                pl.BlockSpec((1, tq, LANE), lambda bi, i: (bi, i, P_SMALL))]
    assert lpad // LANE <= 256
    scratch = [pltpu.VMEM((lpad // tk, tq, tk), jnp.int32), pltpu.VMEM((lpad // tk, tq, tk), jnp.int16),
               pltpu.VMEM((2, tq, tk), F32)]
    if cached:
        cache = kv[0]
        in_specs.append(pl.BlockSpec((1, 1) + cache.shape[2:], lambda bi, i: (layer, bi, 0, 0)))
        in_specs += [pl.BlockSpec((1,) + a.shape[1:], lambda bi, i: (bi, 0, 0)) for a in kv[1:]]
        args = kv
        scratch += [pltpu.VMEM((a.shape[1], lpad), BF16) for a in kv[1:]]
    else:
        k, k_blk, v, v_blk, ik, ik_blk = kv
        in_specs += [pl.BlockSpec((1, n_keys, LANE), lambda bi, i: (bi, 0, k_blk)),
                     pl.BlockSpec((1, n_keys, LANE), lambda bi, i: (bi, 0, v_blk)),
                     pl.BlockSpec((1, n_keys, LANE), lambda bi, i: (bi, 0, ik_blk))]
        args = (k, v, ik)
    return pl.pallas_call(
        body, grid=(b, t // tq), in_specs=in_specs,
        out_specs=pl.BlockSpec((1, tq, B_HEADS * LANE), lambda bi, i: (bi, i, 0)),
        out_shape=jax.ShapeDtypeStruct((b, t, B_HEADS * LANE), BF16),
        scratch_shapes=scratch + _softmax_scratch(rows, tk),
        compiler_params=_params(2), name="dsa_decode" if cached else "dsa_prefill")(q, iq, iw, *args)


def _merge_body(ya_ref, yb_ref, yc_ref, yd_ref, g_ref, wb_ref, o_ref):
    d = o_ref.shape[1]
    acc = 0.0
    for m, y_ref in enumerate((ya_ref, yb_ref, yc_ref, yd_ref)):
        acc = acc + g_ref[:, m * d:(m + 1) * d].astype(F32) * jnp.dot(
            y_ref[...], wb_ref[m], preferred_element_type=F32)
    o_ref[...] = acc.astype(BF16)


def _merge(ys, gate, w_branch):
    m, d = gate.shape[0], w_branch.shape[2]
    tm = _tile(m, 256)
    y_spec = pl.BlockSpec((tm, BR_W), lambda i: (i, 0))
    return pl.pallas_call(
        _merge_body, grid=(m // tm,),
        in_specs=[y_spec] * N_BRANCH + [pl.BlockSpec((tm, N_BRANCH * d), lambda i: (i, 0)),
                                        pl.BlockSpec(w_branch.shape, lambda i: (0, 0, 0))],
        out_specs=pl.BlockSpec((tm, d), lambda i: (i, 0)),
        out_shape=jax.ShapeDtypeStruct((m, d), BF16),
        compiler_params=_params(1), name="merge")(*ys, gate, w_branch)


def _layernorm(z, g, b):
    mu = jnp.mean(z, axis=-1, keepdims=True)
    zc = z - mu
    var = jnp.mean(zc * zc, axis=-1, keepdims=True)
    return zc * lax.rsqrt(var + LN_EPS) * g + b


def _wo_ln_body(m_ref, x_ref, w_ref, g_ref, b_ref, of_ref, ob_ref, *, alpha):
    h = jnp.dot(m_ref[...], w_ref[...], preferred_element_type=F32)
    y = _layernorm(alpha * x_ref[...] + h, g_ref[...], b_ref[...])
    of_ref[...] = y
    ob_ref[...] = y.astype(BF16)


def _wo_ln(merged, x, w_o, g, b, alpha):
    m, d = x.shape
    tm = _tile(m, 256)
    row = pl.BlockSpec((tm, d), lambda i: (i, 0))
    vec = pl.BlockSpec((1, d), lambda i: (0, 0))
    return pl.pallas_call(
        functools.partial(_wo_ln_body, alpha=alpha), grid=(m // tm,),
        in_specs=[row, row, pl.BlockSpec((d, d), lambda i: (0, 0)), vec, vec],
        out_specs=[row, row],
        out_shape=[jax.ShapeDtypeStruct((m, d), F32), jax.ShapeDtypeStruct((m, d), BF16)],
        compiler_params=_params(1), name="wo_ln")(merged, x, w_o, g, b)


def _ffn_body(xb_ref, xf_ref, wu_ref, wd_ref, g_ref, b_ref, of_ref, ob_ref, acc, *, alpha):
    f = pl.program_id(1)

    @pl.when(f == 0)
    def _():
        acc[...] = jnp.zeros_like(acc)

    h = jnp.maximum(jnp.dot(xb_ref[...], wu_ref[...], preferred_element_type=F32), 0.0)
    acc[...] += jnp.dot((h * h).astype(BF16), wd_ref[...], preferred_element_type=F32)

    @pl.when(f == pl.num_programs(1) - 1)
    def _():
        y = _layernorm(alpha * xf_ref[...] + acc[...], g_ref[...], b_ref[...])
        of_ref[...] = y
        ob_ref[...] = y.astype(BF16)


def _ffn(xb, xf, w_up, w_down, g, b, alpha):
    m, d = xf.shape
    dff = w_up.shape[1]
    tm = _tile(m, 512)
    tf = _tile(dff, 512, LANE)
    row = pl.BlockSpec((tm, d), lambda i, f: (i, 0))
    vec = pl.BlockSpec((1, d), lambda i, f: (0, 0))
    return pl.pallas_call(
        functools.partial(_ffn_body, alpha=alpha), grid=(m // tm, dff // tf),
        in_specs=[row, row, pl.BlockSpec((d, tf), lambda i, f: (0, f)),
                  pl.BlockSpec((tf, d), lambda i, f: (f, 0)), vec, vec],
        out_specs=[row, row],
        out_shape=[jax.ShapeDtypeStruct((m, d), F32), jax.ShapeDtypeStruct((m, d), BF16)],
        scratch_shapes=[pltpu.VMEM((tm, d), F32)],
        compiler_params=_params(2), name="ffn")(xb, xf, w_up, w_down, g, b)


def _prep_weights(w_in, w_branch, w_o, w_up, w_down):
    d = w_in.shape[1]
    sizes = (BR_W, BR_W, BR_W, B_HEADS * HD, HD, HD, IDX_HEADS * IDX_HD, IDX_HD, IDX_HEADS,
             C_HEADS * HD, C_HEADS * HD, C_HEADS * HD, C_HEADS,
             D_HEADS * 2 * D_QK, D_HEADS * 2 * D_QK, D_HEADS * D_V, N_BRANCH * d)
    names = ("a_b", "a_c", "a_h", "b_q", "b_k", "b_v", "b_iq", "b_ik", "b_iw",
             "c_q", "c_k", "c_v", "c_f", "d_q", "d_k", "d_v", "gate")
    starts = np.concatenate([[0], np.cumsum(sizes)])
    col = {n: w_in[:, :, int(starts[i]):int(starts[i + 1])] for i, n in enumerate(names)}
    depth = w_in.shape[0]

    def zeros(n):
        return jnp.zeros((depth, d, n), w_in.dtype)

    w_a = jnp.concatenate([col["a_b"], col["a_c"], col["a_h"]], axis=2)
    w_r = jnp.concatenate([col["b_q"], col["b_iq"], col["b_k"], col["b_ik"], zeros(LANE - IDX_HD),
                           col["d_q"], col["d_k"]], axis=2)
    w_p = jnp.concatenate([col["c_q"], col["c_k"], col["c_v"], col["d_v"], col["b_v"],
                           col["b_iw"], col["c_f"], zeros(LANE - IDX_HEADS - C_HEADS)], axis=2)
    assert w_r.shape[2] == R_BLOCKS * LANE and w_p.shape[2] == P_BLOCKS * LANE
    cast = lambda w: w.astype(BF16)
    return dict(w_a=cast(w_a), w_r=cast(w_r), w_p=cast(w_p), w_g=cast(col["gate"]),
                w_branch=cast(w_branch), w_o=cast(w_o), w_up=cast(w_up), w_down=cast(w_down))


def _cols(a, block, n_blocks=1):
    return a[:, :, block * LANE:(block + n_blocks) * LANE]


def _trunk(x, past_len, conv_state, dsa_rows, fox_rows, diff_rows, wts, conv_w, fox_fbias,
           diff_lambda, diff_gain, ln1_g, ln1_b, ln2_g, ln2_b):
    b, t, d = x.shape
    depth = conv_w.shape[0]
    alpha = (2 * depth) ** 0.25
    m = b * t
    n_keys = past_len + t
    pos = jnp.tile(past_len + jnp.arange(t, dtype=jnp.int32), b)
    tab = _rope_tables(pos)
    if past_len:
        fox_cache = jnp.transpose(fox_rows, (0, 1, 4, 3, 2))
        dsa_cache = jnp.transpose(dsa_rows, (0, 1, 3, 2))
        tq_dsa, tk_dsa = t, -(-n_keys // LANE) * LANE
        new_pad = tk_dsa - past_len

        def new_t(a):
            return jnp.pad(jnp.transpose(a, (0, 2, 1)), ((0, 0), (0, 0), (0, new_pad - t))).astype(BF16)
    else:
        tq_att = _tile(t, 512)
        tk_att = _tile(t, 512, LANE)
        tq_dsa, tk_dsa = _tile(t, 128), tk_att

    xf = x.reshape(m, d)
    xb = xf.astype(BF16)
    conv_out, dsa_out, fox_out, diff_out = [], [], [], []
    for l in range(depth):
        lam_init = 0.8 - 0.6 * float(np.exp(-0.3 * l))
        u_a = _matmul("proj_conv", xb, wts["w_a"][l])[0].reshape(b, t, 3 * BR_W)
        rf, rb = _matmul("proj_rope", xb, wts["w_r"][l], tab=tab, kinds=R_KINDS, scales=R_SCALES,
                         want_b16=True)
        pf, pb = _matmul("proj_plain", xb, wts["w_p"][l], kinds=P_KINDS, scales=P_SCALES, want_b16=True)
        (gate,) = _matmul("proj_gate", xb, wts["w_g"][l], sigmoid=True, want_f32=False, want_b16=True,
                          tm_target=512)
        rf, rb, pf, pb = (a.reshape(b, t, a.shape[1]) for a in (rf, rb, pf, pb))

        y_a, n_conv = _short_conv(u_a, conv_state[l], conv_w[l])

        small = _cols(pf, P_SMALL)
        bias_row = jnp.zeros((1, LANE), F32).at[0, CF_LANE:CF_LANE + C_HEADS].set(fox_fbias[l])
        if past_len:
            past_lf = jnp.transpose(fox_cache[l, :, 2 * HD], (0, 2, 1))
            past_lf = jnp.pad(past_lf, ((0, 0), (0, 0), (CF_LANE, LANE - CF_LANE - C_HEADS)))
            small = jnp.concatenate([past_lf, small], axis=1)
        lf, cum = _fox_prep(small, bias_row, past_len)
        lf_new = lf[:, past_len:, CF_LANE:CF_LANE + C_HEADS]
        cum = cum[:, :, CF_LANE:CF_LANE + C_HEADS]
        diff_extra = (diff_lambda[l], diff_gain[l].reshape(1, D_V), jnp.full((1, 1), lam_init, F32))
        if past_len:
            ck = jnp.pad(jnp.transpose(cum, (0, 2, 1)), ((0, 0), (0, 0), (0, past_len + LANE - n_keys)))
            y_c = _decode("fox", l, pb, P_CQ // 4, fox_cache, pb, P_CK // 4, pb, P_CV // 4,
                          (ck[:, :, None, :],), past_len=past_len)
            y_d = _decode("diff", l, rb, R_DQ // 4, diff_rows, rb, R_DK // 4, pb, P_DV // 4,
                          diff_extra, past_len=past_len)
            dsa_kv = (dsa_cache, new_t(_cols(rf, R_BK)), new_t(_cols(pf, P_BV)),
                      new_t(_cols(rf, R_BIK)[..., :IDX_HD]))
        else:
            ck = jnp.transpose(cum, (0, 2, 1)).reshape(b, C_HEADS, t // tk_att, 1, tk_att)
            y_c = _flash("fox", pb, P_CQ, pb, P_CK, pb, P_CV, (ck,), tq=tq_att, tk=tk_att)
            y_d = _flash("diff", rb, R_DQ, rb, R_DK, pb, P_DV, diff_extra, tq=tq_att, tk=tk_att)
            dsa_kv = (rb, R_BK, pb, P_BV, rb, R_BIK)
        y_b = _dsa(rb, rf, pf, dsa_kv, layer=l, past_len=past_len, n_keys=n_keys, tq=tq_dsa, tk=tk_dsa)

        ys = [y.reshape(m, BR_W) for y in (y_a, y_b, y_c, y_d)]
        merged = _merge(ys, gate, wts["w_branch"][l])
        xf, xb = _wo_ln(merged, xf, wts["w_o"][l], ln1_g[l].reshape(1, d), ln1_b[l].reshape(1, d), alpha)
        xf, xb = _ffn(xb, xf, wts["w_up"][l], wts["w_down"][l], ln2_g[l].reshape(1, d),
                      ln2_b[l].reshape(1, d), alpha)

        conv_out.append(n_conv)
        dsa_out.append(jnp.concatenate([_cols(rf, R_BK), _cols(pf, P_BV), _cols(rf, R_BIK)[..., :IDX_HD]],
                                       axis=-1))
        fox_out.append(jnp.concatenate([_cols(pf, P_CK, 4).reshape(b, t, C_HEADS, HD),
                                        _cols(pf, P_CV, 4).reshape(b, t, C_HEADS, HD),
                                        lf_new[..., None]], axis=-1))
        diff_out.append(jnp.concatenate([_cols(rf, R_DK, 4).reshape(b, t, D_HEADS, 2 * D_QK),
                                         _cols(pf, P_DV, 4).reshape(b, t, D_HEADS, D_V)], axis=-1))
    return (xf.reshape(b, t, d), jnp.stack(conv_out), jnp.stack(dsa_out), jnp.stack(fox_out),
            jnp.stack(diff_out))


def kernel(x_prompt, x_sample, state_conv, cache_dsa, cache_fox, cache_diff, w_in, conv_w, fox_fbias,
           diff_lambda, diff_gain, w_branch, w_o, ln1_g, ln1_b, w_up, w_down, ln2_g, ln2_b):
    depth = w_in.shape[0]
    b = x_prompt.shape[0]
    wts = _prep_weights(w_in, w_branch, w_o, w_up, w_down)
    shared = (wts, conv_w, fox_fbias, diff_lambda, diff_gain, ln1_g, ln1_b, ln2_g, ln2_b)
    conv0 = jnp.zeros((depth, b, CONV_W - 1, BR_W), x_prompt.dtype)
    y_p, conv_p, dsa_p, fox_p, diff_p = _trunk(x_prompt, 0, conv0, None, None, None, *shared)
    y_s, conv_s, dsa_s, fox_s, diff_s = _trunk(x_sample, cache_dsa.shape[2], state_conv, cache_dsa,
                                               cache_fox, cache_diff, *shared)
    return (y_p, y_s, conv_p, dsa_p, fox_p, diff_p, conv_s, dsa_s, fox_s, diff_s)
```

```python
import functools

import numpy as np
import jax
import jax.numpy as jnp
from jax import lax
from jax.experimental import pallas as pl
from jax.experimental.pallas import tpu as pltpu

CHUNK = 64
N_BRANCH = 4
BR_W = 512
HD = 128
CONV_W = 3
B_HEADS = 4
IDX_HEADS = 4
IDX_HD = 64
TOPK_MAX = 256
IDX_SCALE = (IDX_HEADS * IDX_HD) ** -0.5
C_HEADS = 4
D_HEADS = 4
D_QK = 64
D_V = 128
ROPE_THETA = 500000.0
LN_EPS = 1e-5
RMS_EPS = 1e-5

LANE = 128
STRIP = 16
VMEM_LIMIT = 56 * 1024 * 1024
NEG = -1e30
LOG2E = 1.4426950408889634
INT_MIN = np.int32(-2 ** 31)
F32 = jnp.float32
BF16 = jnp.bfloat16

R_BQ, R_BIQ, R_BK, R_BIK, R_DQ, R_DK, R_BLOCKS = 0, 4, 6, 7, 8, 12, 16
P_CQ, P_CK, P_CV, P_DV, P_BV, P_SMALL, P_BLOCKS = 0, 4, 8, 12, 16, 17, 18
IW_LANE, CF_LANE = 0, IDX_HEADS
R_KINDS = ("r128",) * 4 + ("r64",) * 2 + ("r128", "r64") + ("r64",) * 8
R_SCALES = (HD ** -0.5 * LOG2E,) * 4 + (1.0,) * 4 + (D_QK ** -0.5 * LOG2E,) * 4 + (1.0,) * 4
P_KINDS = (None,) * P_BLOCKS
P_SCALES = (HD ** -0.5 * LOG2E,) * 4 + (1.0,) * 14


def _tile(n, target, align=8):
    if n <= target:
        return n
    for t in range(target - target % align, 0, -align):
        if n % t == 0:
            return t
    return n


def _params(n_grid):
    return pltpu.CompilerParams(dimension_semantics=("arbitrary",) * n_grid,
                                vmem_limit_bytes=VMEM_LIMIT)


def _mm_body(x_ref, w_ref, *rest, kinds, scales, has_tab, sigmoid, want_f32, want_b16):
    tab_ref = rest[0] if has_tab else None
    outs = rest[1:] if has_tab else rest
    of_ref = outs[0] if want_f32 else None
    ob_ref = outs[-1] if want_b16 else None
    acc = jnp.dot(x_ref[...], w_ref[...], preferred_element_type=F32)
    if kinds is None:
        if sigmoid:
            acc = jax.nn.sigmoid(acc)
        if want_f32:
            of_ref[...] = acc
        if want_b16:
            ob_ref[...] = acc.astype(BF16)
        return
    for c, (kind, scale) in enumerate(zip(kinds, scales)):
        sl = slice(c * LANE, (c + 1) * LANE)
        blk = acc[:, sl]
        if kind is not None:
            base = 0 if kind == "r128" else 3 * LANE
            half = (HD // 4) // 2 if kind == "r128" else (IDX_HD // 4) // 2
            cos = tab_ref[:, base:base + LANE]
            s_lo = tab_ref[:, base + LANE:base + 2 * LANE]
            s_hi = tab_ref[:, base + 2 * LANE:base + 3 * LANE]
            blk = (blk * cos + pltpu.roll(blk, LANE - half, 1) * s_lo
                   + pltpu.roll(blk, half, 1) * s_hi)
        if want_f32:
            of_ref[:, sl] = blk
        if want_b16:
            ob_ref[:, sl] = (blk * scale if scale != 1.0 else blk).astype(BF16)


def _matmul(name, x, w, *, tab=None, kinds=None, scales=None, sigmoid=False,
            want_f32=True, want_b16=False, tm_target=256, tn_target=2048):
    m, k = x.shape
    n = w.shape[1]
    tm = _tile(m, tm_target)
    tn = n if kinds is not None else _tile(n, tn_target, LANE)
    grid = (n // tn, m // tm)
    in_specs = [pl.BlockSpec((tm, k), lambda j, i: (i, 0)),
                pl.BlockSpec((k, tn), lambda j, i: (0, j))]
    args = [x, w]
    if tab is not None:
        in_specs.append(pl.BlockSpec((tm, tab.shape[1]), lambda j, i: (i, 0)))
        args.append(tab)
    out_shape, out_specs = [], []
    if want_f32:
        out_shape.append(jax.ShapeDtypeStruct((m, n), F32))
        out_specs.append(pl.BlockSpec((tm, tn), lambda j, i: (i, j)))
    if want_b16:
        out_shape.append(jax.ShapeDtypeStruct((m, n), BF16))
        out_specs.append(pl.BlockSpec((tm, tn), lambda j, i: (i, j)))
    body = functools.partial(_mm_body, kinds=kinds, scales=scales, has_tab=tab is not None,
                             sigmoid=sigmoid, want_f32=want_f32, want_b16=want_b16)
    return pl.pallas_call(body, grid=grid, in_specs=in_specs, out_specs=out_specs,
                          out_shape=out_shape, compiler_params=_params(2), name=name)(*args)


def _rope_tables(pos):
    tabs = []
    for hd in (HD, IDX_HD):
        rot = hd // 4
        half = rot // 2
        freqs = ROPE_THETA ** (-2.0 * jnp.arange(half, dtype=F32) / rot)
        ang = pos.astype(F32)[:, None] * freqs[None, :]
        cos, sin = jnp.cos(ang), jnp.sin(ang)
        m = pos.shape[0]
        one = jnp.ones((m, hd - rot), F32)
        zero_h = jnp.zeros((m, half), F32)
        zero_t = jnp.zeros((m, hd - rot), F32)
        c = jnp.concatenate([cos, cos, one], axis=1)
        s_lo = jnp.concatenate([-sin, zero_h, zero_t], axis=1)
        s_hi = jnp.concatenate([zero_h, sin, zero_t], axis=1)
        rep = LANE // hd
        tabs += [jnp.tile(c, (1, rep)), jnp.tile(s_lo, (1, rep)), jnp.tile(s_hi, (1, rep))]
    return jnp.concatenate(tabs, axis=1)


def _conv_body(u_ref, st_ref, w_ref, y_ref, ns_ref, carry, *, tm):
    @pl.when(pl.program_id(1) == 0)
    def _():
        carry[...] = st_ref[0]

    gate_b = u_ref[0, :, 0:BR_W]
    v = u_ref[0, :, BR_W:2 * BR_W] * u_ref[0, :, 2 * BR_W:3 * BR_W]
    p0 = carry[0:1, :]
    p1 = carry[1:2, :]
    row = lax.broadcasted_iota(jnp.int32, (tm, BR_W), 0)
    v1 = jnp.where(row == 0, p1, pltpu.roll(v, 1, 0))
    v2 = jnp.where(row == 0, p0, jnp.where(row == 1, p1, pltpu.roll(v, 2, 0)))
    w = w_ref[...]
    y = gate_b * (v2 * w[0:1, :] + v1 * w[1:2, :] + v * w[2:3, :])
    y_ref[0] = y.astype(BF16)
    tail = (u_ref[0, tm - 2:tm, BR_W:2 * BR_W] * u_ref[0, tm - 2:tm, 2 * BR_W:3 * BR_W])
    carry[...] = tail
    ns_ref[0] = tail


def _short_conv(u_a, state, conv_w):
    b, t, _ = u_a.shape
    assert t >= CONV_W - 1
    tm = _tile(t, 512)
    body = functools.partial(_conv_body, tm=tm)
    return pl.pallas_call(
        body, grid=(b, t // tm),
        in_specs=[pl.BlockSpec((1, tm, 3 * BR_W), lambda bi, i: (bi, i, 0)),
                  pl.BlockSpec((1, CONV_W - 1, BR_W), lambda bi, i: (bi, 0, 0)),
                  pl.BlockSpec((CONV_W, BR_W), lambda bi, i: (0, 0))],
        out_specs=[pl.BlockSpec((1, tm, BR_W), lambda bi, i: (bi, i, 0)),
                   pl.BlockSpec((1, CONV_W - 1, BR_W), lambda bi, i: (bi, 0, 0))],
        out_shape=[jax.ShapeDtypeStruct((b, t, BR_W), BF16),
                   jax.ShapeDtypeStruct((b, CONV_W - 1, BR_W), F32)],
        scratch_shapes=[pltpu.VMEM((CONV_W - 1, BR_W), F32)],
        compiler_params=_params(2), name="short_conv")(u_a, state, conv_w)


def _fox_prep_body(x_ref, bias_ref, lf_ref, cum_ref, carry, *, tl, past_len):
    i = pl.program_id(1)

    @pl.when(i == 0)
    def _():
        carry[...] = jnp.zeros_like(carry)

    x = x_ref[0]
    z = x + bias_ref[...]
    log_sig = jnp.minimum(z, 0.0) - jnp.log1p(jnp.exp(-jnp.abs(z)))
    row = i * tl + lax.broadcasted_iota(jnp.int32, (tl, 1), 0)
    lf = jnp.where(row >= past_len, log_sig, x)
    r = lax.broadcasted_iota(jnp.int32, (tl, tl), 0)
    c = lax.broadcasted_iota(jnp.int32, (tl, tl), 1)
    tri = jnp.where(c <= r, 1.0, 0.0).astype(F32)
    cs = jnp.dot(tri, lf, precision=lax.Precision.HIGHEST, preferred_element_type=F32) + carry[...]
    lf_ref[0] = lf
    cum_ref[0] = cs
    carry[...] = cs[tl - 1:tl, :]


def _fox_prep(x, bias_row, past_len):
    b, l, _ = x.shape
    tl = _tile(l, 512)
    body = functools.partial(_fox_prep_body, tl=tl, past_len=past_len)
    spec = pl.BlockSpec((1, tl, LANE), lambda bi, i: (bi, i, 0))
    return pl.pallas_call(
        body, grid=(b, l // tl),
        in_specs=[spec, pl.BlockSpec((1, LANE), lambda bi, i: (0, 0))],
        out_specs=[spec, spec],
        out_shape=[jax.ShapeDtypeStruct((b, l, LANE), F32)] * 2,
        scratch_shapes=[pltpu.VMEM((1, LANE), F32)],
        compiler_params=_params(2), name="fox_prep")(x, bias_row)


def _row_limits(qpos, chunk):
    if chunk == 1:
        return qpos + 1
    sh = chunk.bit_length() - 1
    return lax.shift_left(lax.shift_right_arithmetic(qpos, sh) + 1, sh)


def _tile_counts(q0, tq, tk, chunk, n_keys):
    lim_min = (q0 // chunk + 1) * chunk
    lim_max = ((q0 + tq - 1) // chunk + 1) * chunk
    n_full = jnp.minimum(lim_min, n_keys) // tk
    n_need = (jnp.minimum(lim_max, n_keys) + tk - 1) // tk
    return n_full, n_need


def _softmax_scratch(rows, tk):
    return [pltpu.VMEM((rows, LANE), F32)] * 4 + [pltpu.VMEM((2, rows, tk), BF16)]


def _softmax_init(m_sc, l_sc, acc_sc):
    m_sc[...] = jnp.full(m_sc.shape, NEG, F32)
    l_sc[...] = jnp.zeros(l_sc.shape, F32)
    acc_sc[...] = jnp.zeros(acc_sc.shape, F32)


def _softmax_tile(s, pv, state, fix=None, base=0, slot=0):
    m_sc, l_sc, a_sc, acc_sc, p_sc = state
    rows, tk = s.shape
    for r0 in range(0, rows, STRIP):
        rs = slice(base + r0, base + r0 + STRIP)
        x = s[r0:r0 + STRIP, :]
        if fix is not None:
            x = fix(x, r0)
        blocks = [x[:, c * LANE:(c + 1) * LANE] for c in range(tk // LANE)]
        mx = blocks[0]
        for blk in blocks[1:]:
            mx = jnp.maximum(mx, blk)
        m_old = m_sc[rs, :]
        m_new = jnp.maximum(m_old, jnp.max(mx, axis=1, keepdims=True))
        alpha = jnp.exp2(m_old - m_new)
        tot = None
        for c, blk in enumerate(blocks):
            p = jnp.exp2(blk - m_new)
            tot = p if tot is None else tot + p
            p_sc[slot, rs, c * LANE:(c + 1) * LANE] = p.astype(BF16)
        l_sc[rs, :] = alpha * l_sc[rs, :] + tot
        m_sc[rs, :] = m_new
        a_sc[rs, :] = alpha
    rs = slice(base, base + rows)
    acc_sc[rs, :] = a_sc[rs, :] * acc_sc[rs, :] + pv(p_sc[slot, rs, 0:tk])


def _pv(v):
    return lambda p: jnp.dot(p, v, preferred_element_type=F32)


def _softmax_out(state):
    _, l_sc, _, acc_sc, _ = state
    return acc_sc[...] / jnp.sum(l_sc[...], axis=1, keepdims=True)


def _qk(q, k):
    return lax.dot_general(q, k, (((1,), (1,)), ((), ())), preferred_element_type=F32)


def _split_subheads(q):
    qf = q.astype(F32)
    lane = lax.broadcasted_iota(jnp.int32, qf.shape, 1)
    return jnp.concatenate([jnp.where(lane < D_QK, qf, 0.0), jnp.where(lane >= D_QK, qf, 0.0)],
                           axis=0).astype(BF16)


def _diff_out(o, tq, lam_ref, gain_ref, li_ref):
    lam = lam_ref[...]
    li = li_ref[...]
    lmb = (jnp.exp(jnp.sum(lam[0:1, :] * lam[1:2, :], axis=1, keepdims=True))
           - jnp.exp(jnp.sum(lam[2:3, :] * lam[3:4, :], axis=1, keepdims=True)) + li)
    o = o[0:tq, :] - lmb * o[tq:2 * tq, :]
    o = o * lax.rsqrt(jnp.mean(o * o, axis=1, keepdims=True) + RMS_EPS)
    return o * gain_ref[...] * (1.0 - li)


def _flash_body(*refs, tq, tk, chunk, mode):
    if mode == "fox":
        q_ref, k_ref, v_ref, ck_ref = refs[:4]
        o_ref, state = refs[4], refs[5:]
    else:
        q_ref, k_ref, v_ref, lam_ref, gain_ref, li_ref = refs[:6]
        o_ref, state = refs[6], refs[7:]
    n_keys = k_ref.shape[1]
    q0 = pl.program_id(2) * tq
    n_full, n_need = _tile_counts(q0, tq, tk, chunk, n_keys)
    qlim = _row_limits(q0 + lax.broadcasted_iota(jnp.int32, (tq, 1), 0), chunk)
    if mode == "fox":
        q = q_ref[0]
    else:
        q = _split_subheads(q_ref[0])
        qlim = jnp.concatenate([qlim, qlim], axis=0)
    _softmax_init(state[0], state[1], state[3])

    def step(j, masked, slot):
        ks = pl.multiple_of(j * tk, tk)
        s = _qk(q, k_ref[0, pl.ds(ks, tk), :])
        ck = ck_ref[0, 0, j] * LOG2E if mode == "fox" else None
        kpos = ks + lax.broadcasted_iota(jnp.int32, (1, tk), 1)

        def fix(x, r0):
            if ck is not None:
                x = x - ck
            if masked:
                x = jnp.where(kpos < qlim[r0:r0 + STRIP, :], x, NEG)
            return x

        _softmax_tile(s, _pv(v_ref[0, pl.ds(ks, tk), :]), state,
                      fix if (masked or ck is not None) else None, slot=slot)

    def full_pair(jj, c):
        step(2 * jj, False, 0)
        step(2 * jj + 1, False, 1)
        return c

    def masked_step(j, c):
        step(j, True, 0)
        return c

    lax.fori_loop(0, n_full // 2, full_pair, 0)

    @pl.when(n_full % 2 == 1)
    def _():
        step(n_full - 1, False, 0)

    lax.fori_loop(n_full, n_need, masked_step, 0)

    o = _softmax_out(state)
    if mode == "diff":
        o = _diff_out(o, tq, lam_ref, gain_ref, li_ref)
    o_ref[0] = o.astype(BF16)


def _flash(mode, q, q_off, k, k_off, v, v_off, extra, *, tq, tk):
    b, t, _ = q.shape
    heads = C_HEADS if mode == "fox" else D_HEADS
    stack = 1 if mode == "fox" else 2
    chunk = 1 if mode == "fox" else CHUNK
    in_specs = [pl.BlockSpec((1, tq, LANE), lambda bi, h, i: (bi, i, q_off + h)),
                pl.BlockSpec((1, t, LANE), lambda bi, h, i: (bi, 0, k_off + h)),
                pl.BlockSpec((1, t, LANE), lambda bi, h, i: (bi, 0, v_off + h))]
    if mode == "fox":
        in_specs.append(pl.BlockSpec((1, 1, t // tk, 1, tk), lambda bi, h, i: (bi, h, 0, 0, 0)))
    else:
        in_specs += [pl.BlockSpec(e.shape, lambda bi, h, i: (0, 0)) for e in extra]
    body = functools.partial(_flash_body, tq=tq, tk=tk, chunk=chunk, mode=mode)
    return pl.pallas_call(
        body, grid=(b, heads, t // tq), in_specs=in_specs,
        out_specs=pl.BlockSpec((1, tq, LANE), lambda bi, h, i: (bi, i, h)),
        out_shape=jax.ShapeDtypeStruct((b, t, heads * LANE), BF16),
        scratch_shapes=_softmax_scratch(stack * tq, tk),
        compiler_params=_params(3), name=mode + "_prefill")(q, k, v, *extra)


def _decode_body(*refs, t, past_len, mode):
    if mode == "fox":
        q_ref, cache_ref, kn_ref, vn_ref, ck_ref = refs[:5]
        o_ref, state = refs[5], refs[6:]
    else:
        q_ref, cache_ref, kn_ref, vn_ref, lam_ref, gain_ref, li_ref = refs[:7]
        o_ref, state = refs[7], refs[8:]
    heads = C_HEADS
    stack = 1 if mode == "fox" else 2
    rows = stack * t
    tt = lax.broadcasted_iota(jnp.int32, (t, 1), 0)
    qlim = jnp.minimum(_row_limits(past_len + tt, 1 if mode == "fox" else CHUNK), past_len + t)
    qlim = jnp.concatenate([qlim] * stack, axis=0)
    new_pos = past_len + lax.broadcasted_iota(jnp.int32, (1, LANE), 1)
    zpad = jnp.zeros((LANE - t, LANE), BF16)
    _softmax_init(state[0], state[1], state[3])

    for h in range(heads):
        hs = slice(h * LANE, (h + 1) * LANE)
        base = h * rows
        qh = q_ref[0, :, hs]
        if mode == "fox":
            kt = cache_ref[0, 0, 0:HD, h, :].astype(BF16)
            vt = cache_ref[0, 0, HD:2 * HD, h, :].astype(BF16)
            ck_past = ck_ref[0, h, :, 0:past_len] * LOG2E
            ck_new = ck_ref[0, h, :, past_len:past_len + LANE] * LOG2E
            s = jnp.dot(qh, kt, preferred_element_type=F32)
            _softmax_tile(s, lambda p, vt=vt: _qk(p, vt), state,
                          lambda x, r0, ck_past=ck_past: x - ck_past, base)
        else:
            qh = _split_subheads(qh)
            k = cache_ref[0, 0, :, h, 0:2 * D_QK].astype(BF16)
            v = cache_ref[0, 0, :, h, 2 * D_QK:].astype(BF16)
            ck_new = None
            _softmax_tile(_qk(qh, k), _pv(v), state, None, base)

        def fix(x, r0, ck_new=ck_new):
            if ck_new is not None:
                x = x - ck_new
            return jnp.where(new_pos < qlim[r0:r0 + STRIP, :], x, NEG)

        kn = jnp.concatenate([kn_ref[0, :, hs], zpad], axis=0)
        vn = jnp.concatenate([vn_ref[0, :, hs], zpad], axis=0)
        _softmax_tile(_qk(qh, kn), _pv(vn), state, fix, base)

    o = _softmax_out(state)
    for h in range(heads):
        oh = o[h * rows:(h + 1) * rows, :]
        if mode == "diff":
            oh = _diff_out(oh, t, lam_ref, gain_ref, li_ref)
        o_ref[0, :, h * LANE:(h + 1) * LANE] = oh.astype(BF16)


def _decode(mode, layer, q, q_blk, cache, k_new, k_blk, v_new, v_blk, extra, *, past_len):
    b, t, _ = q.shape
    heads = C_HEADS
    assert t <= LANE and t % STRIP == 0 and past_len % LANE == 0
    stack = 1 if mode == "fox" else 2
    wide = lambda blk: pl.BlockSpec((1, t, heads * LANE), lambda bi: (bi, 0, blk))
    in_specs = [wide(q_blk), pl.BlockSpec((1, 1) + cache.shape[2:], lambda bi: (layer, bi, 0, 0, 0)),
                wide(k_blk), wide(v_blk)]
    if mode == "fox":
        in_specs.append(pl.BlockSpec((1,) + extra[0].shape[1:], lambda bi: (bi, 0, 0, 0)))
    else:
        in_specs += [pl.BlockSpec(e.shape, lambda bi: (0, 0)) for e in extra]
    body = functools.partial(_decode_body, t=t, past_len=past_len, mode=mode)
    return pl.pallas_call(
        body, grid=(b,), in_specs=in_specs, out_specs=wide(0),
        out_shape=jax.ShapeDtypeStruct((b, t, heads * LANE), BF16),
        scratch_shapes=_softmax_scratch(stack * heads * t, past_len),
        compiler_params=_params(1), name=mode + "_decode")(q, cache, k_new, v_new, *extra)


def _dsa_body(*refs, tq, tk, past_len, n_keys, k_sel, cached):
    if cached:
        (q_ref, iq_ref, iw_ref, cache_ref, knt_ref, vnt_ref, iknt_ref, o_ref,
         keys, keys_hi, bias_sc, kt_all, vt_all, ikt_all) = refs[:14]
        state = refs[14:]
        for dst, new, lo, hi in ((kt_all, knt_ref, 0, HD), (vt_all, vnt_ref, HD, 2 * HD),
                                 (ikt_all, iknt_ref, 2 * HD, 2 * HD + IDX_HD)):
            dst[:, 0:past_len] = cache_ref[0, 0, lo:hi, :].astype(BF16)
            dst[:, past_len:] = new[0]
        index_logits = lambda iq, ks: jnp.dot(iq[:, 0:IDX_HD], ikt_all[...], preferred_element_type=F32)
        key_logits = lambda q, ks: jnp.dot(q, kt_all[...], preferred_element_type=F32)
        values = lambda ks: (lambda p: _qk(p, vt_all[...]))
    else:
        q_ref, iq_ref, iw_ref, k_ref, v_ref, ik_ref, o_ref, keys, keys_hi, bias_sc = refs[:10]
        state = refs[10:]
        index_logits = lambda iq, ks: _qk(iq, ik_ref[0, pl.ds(ks, tk), :])
        key_logits = lambda q, ks: _qk(q, k_ref[0, pl.ds(ks, tk), :])
        values = lambda ks: _pv(v_ref[0, pl.ds(ks, tk), :])

    q0 = past_len + pl.program_id(1) * tq
    _, n_need = _tile_counts(q0, tq, tk, CHUNK, n_keys)
    qlim = jnp.minimum(_row_limits(q0 + lax.broadcasted_iota(jnp.int32, (tq, 1), 0), CHUNK), n_keys)

    iqf = iq_ref[0]
    parts = []
    for h in range(IDX_HEADS):
        blk = iqf[:, (h // 2) * LANE:(h // 2 + 1) * LANE]
        parts.append(pltpu.roll(blk, IDX_HD, 1) if h % 2 else blk)
    iq = jnp.concatenate([p[r0:r0 + STRIP, :] for r0 in range(0, tq, STRIP) for p in parts],
                         axis=0).astype(BF16)
    iw = iw_ref[0]
    fold_scale = np.frexp(IDX_SCALE)[0] == 0.5
    w_b = [jnp.broadcast_to(iw[:, IW_LANE + h:IW_LANE + h + 1] * (IDX_SCALE if fold_scale else 1.0),
                            (tq, LANE)) for h in range(IDX_HEADS)]
    qlim_b = jnp.broadcast_to(qlim, (tq, LANE))
    lane = lax.broadcasted_iota(jnp.int32, (1, LANE), 1)

    def score_tile(j, c):
        ks = pl.multiple_of(j * tk, tk)
        rel = index_logits(iq, ks)
        for r0 in range(0, tq, STRIP):
            rs = slice(r0, r0 + STRIP)
            for c0 in range(0, tk, LANE):
                cs = slice(c0, c0 + LANE)
                sc = None
                for h in range(IDX_HEADS):
                    first = (r0 * IDX_HEADS) + h * STRIP
                    term = jnp.maximum(rel[first:first + STRIP, cs], 0.0) * w_b[h][rs, :]
                    sc = term if sc is None else sc + term
                if not fold_scale:
                    sc = sc * IDX_SCALE
                sc = jnp.where(sc == 0.0, 0.0, sc)
                bits = lax.bitcast_convert_type(sc, jnp.int32)
                key = jnp.where(bits < 0, bits ^ jnp.int32(0x7FFFFFFF), bits)
                key = jnp.where(ks + c0 + lane < qlim_b[rs, :], key, INT_MIN)
                keys[j, rs, cs] = key
                keys_hi[j, rs, cs] = lax.shift_right_arithmetic(key, 16).astype(jnp.int16)
        return c

    lax.fori_loop(0, n_need, score_tile, 0)

    def count(pred):
        def add(j, acc):
            hit = jnp.where(pred(keys[j]), 1.0, 0.0)
            for c in range(tk // LANE):
                acc = acc + hit[:, c * LANE:(c + 1) * LANE]
            return acc

        def pair(jj, acc):
            return add(2 * jj + 1, add(2 * jj, acc))

        acc = lax.fori_loop(0, n_need // 2, pair, jnp.zeros((tq, LANE), F32))
        acc = lax.cond(n_need % 2 == 1, lambda a: add(n_need - 1, a), lambda a: a, acc)
        return jnp.sum(acc, axis=1, keepdims=True)

    def count_ge(cand):
        return count(lambda t: t >= cand)

    def count_hi_ge(cand):
        cand = jnp.broadcast_to(cand, (tq, LANE)).astype(jnp.int16)
        one = jnp.ones((tq, LANE), BF16)

        def add(j, acc):
            t = keys_hi[j]
            for c in range(tk // LANE):
                acc = acc + jnp.where(t[:, c * LANE:(c + 1) * LANE] >= cand, one, jnp.zeros_like(one))
            return acc

        acc = lax.fori_loop(0, n_need, add, jnp.zeros((tq, LANE), BF16))
        return jnp.sum(acc.astype(F32), axis=1, keepdims=True)

    def any_row(flag):
        return (jnp.max(jnp.where(flag, 1.0, 0.0)) > 0.0).astype(jnp.int32)

    want = jnp.float32(k_sel)
    crowded = count_ge(jnp.full((tq, 1), INT_MIN + 1, jnp.int32)) > want

    zero = jnp.zeros((tq, 1), jnp.int32)
    n_nonneg = count_ge(zero)
    n_pos = count_ge(zero + 1)
    at_zero = crowded & (n_pos < want) & (n_nonneg >= want)
    searching = crowded & jnp.logical_not(at_zero)
    half = 2 ** 15

    def upper_bit(b, st):
        ans, n_ans = st
        cand = ans + lax.shift_left(jnp.int32(1), 15 - b)
        n = count_hi_ge(cand)
        take = (n >= want) & searching
        return jnp.where(take, cand, ans), jnp.where(take, n, n_ans)

    top, n_top = lax.fori_loop(0, 16, upper_bit, (jnp.full((tq, 1), -half, jnp.int32),
                                                  jnp.full((tq, 1), 2.0 * k_sel + 1.0, F32)))
    go = any_row(searching & (n_top != want))
    n_above = jnp.where(top < half - 1, count_hi_ge(jnp.minimum(top + 1, half - 1)), 0.0)

    @pl.when(go > 0)
    def _():
        top_b = jnp.broadcast_to(top, (tq, LANE)).astype(jnp.int16)
        floor = jnp.full((tq, LANE), -half, jnp.int16)

        def keep_bucket(j, c):
            low = ((keys[j] & jnp.int32(0xFFFF)) - half).astype(jnp.int16)
            upper = keys_hi[j]
            for c0 in range(0, tk, LANE):
                cs = slice(c0, c0 + LANE)
                keys_hi[j, :, cs] = jnp.where(upper[:, cs] == top_b, low[:, cs], floor)
            return c

        lax.fori_loop(0, n_need, keep_bucket, 0)

    def lower_bit(st):
        b, _, ans, n_ans = st
        cand = ans + lax.shift_left(jnp.int32(1), 15 - b)
        n = n_above + count_hi_ge(cand - half)
        take = (n >= want) & searching
        ans = jnp.where(take, cand, ans)
        n_ans = jnp.where(take, n, n_ans)
        return b + 1, any_row(searching & (n_ans != want)), ans, n_ans

    _, _, low, n_low = lax.while_loop(lambda st: (st[0] < 16) & (st[1] > 0), lower_bit,
                                      (jnp.int32(0), go, zero, n_top))
    kstar = jnp.where(at_zero, zero, lax.shift_left(top, 16) + low)
    n_ge = jnp.where(at_zero, n_nonneg, n_low)
    tie = crowded & (n_ge > want)

    @pl.when(any_row(tie) > 0)
    def _():
        n_gt = lax.cond(any_row(tie & jnp.logical_not(at_zero)) > 0, lambda: count_ge(kstar + 1),
                        lambda: jnp.zeros((tq, 1), F32))
        need = want - jnp.where(at_zero, n_pos, n_gt)
        pw = next(w for w in (512, 256, LANE) if tk % w == 0)
        r = lax.broadcasted_iota(jnp.int32, (pw, pw), 0)
        c = lax.broadcasted_iota(jnp.int32, (pw, pw), 1)
        prefix = jnp.where(r <= c, 1.0, 0.0).astype(BF16)

        def drop(j, seen):
            t = keys[j]
            eq = tie & (t == kstar)
            kept = []
            for c0 in range(0, tk, pw):
                eq_c = eq[:, c0:c0 + pw]
                rank = seen + jnp.dot(jnp.where(eq_c, 1.0, 0.0).astype(BF16), prefix,
                                      preferred_element_type=F32)
                kept.append(jnp.where(eq_c & (rank > need), INT_MIN, t[:, c0:c0 + pw]))
                seen = rank[:, pw - 1:pw]
            keys[j] = kept[0] if len(kept) == 1 else jnp.concatenate(kept, axis=1)
            return seen

        lax.fori_loop(0, n_need, drop, jnp.zeros((tq, 1), F32))

    thr = jnp.maximum(kstar, INT_MIN + 1)
    qf = q_ref[0]
    q = jnp.concatenate([qf[:, h * LANE:(h + 1) * LANE] for h in range(B_HEADS)], axis=0)
    _softmax_init(state[0], state[1], state[3])

    def attend(j, slot):
        ks = pl.multiple_of(j * tk, tk)
        bias_sc[slot] = jnp.where(keys[j] >= thr, 0.0, NEG)
        _softmax_tile(key_logits(q, ks), values(ks), state,
                      lambda x, r0: x + bias_sc[slot, r0 % tq:r0 % tq + STRIP, :], slot=slot)

    def attend_pair(jj, c):
        attend(2 * jj, 0)
        attend(2 * jj + 1, 1)
        return c

    lax.fori_loop(0, n_need // 2, attend_pair, 0)

    @pl.when(n_need % 2 == 1)
    def _():
        attend(n_need - 1, 0)

    o = _softmax_out(state)
    for h in range(B_HEADS):
        o_ref[0, :, h * LANE:(h + 1) * LANE] = o[h * tq:(h + 1) * tq, :].astype(BF16)


def _dsa(q, iq, iw, kv, *, layer, past_len, n_keys, tq, tk):
    b, t, _ = q.shape
    cached = past_len > 0
    lpad = -(-n_keys // tk) * tk
    assert not cached or (lpad == tk and past_len % LANE == 0)
    k_sel = min(TOPK_MAX, n_keys // 4)
    body = functools.partial(_dsa_body, tq=tq, tk=tk, past_len=past_len, n_keys=n_keys, k_sel=k_sel,
                             cached=cached)
    rows = B_HEADS * tq
    in_specs = [pl.BlockSpec((1, tq, B_HEADS * LANE), lambda bi, i: (bi, i, R_BQ // B_HEADS)),
                pl.BlockSpec((1, tq, 2 * LANE), lambda bi, i: (bi, i, R_BIQ // 2)),
                pl.BlockSpec((1, tq, LANE), lambda bi, i: (bi, i, P_SMALL))]
    assert lpad // LANE <= 256
    scratch = [pltpu.VMEM((lpad // tk, tq, tk), jnp.int32), pltpu.VMEM((lpad // tk, tq, tk), jnp.int16),
               pltpu.VMEM((2, tq, tk), F32)]
    if cached:
        cache = kv[0]
        in_specs.append(pl.BlockSpec((1, 1) + cache.shape[2:], lambda bi, i: (layer, bi, 0, 0)))
        in_specs += [pl.BlockSpec((1,) + a.shape[1:], lambda bi, i: (bi, 0, 0)) for a in kv[1:]]
        args = kv
        scratch += [pltpu.VMEM((a.shape[1], lpad), BF16) for a in kv[1:]]
    else:
        k, k_blk, v, v_blk, ik, ik_blk = kv
        in_specs += [pl.BlockSpec((1, n_keys, LANE), lambda bi, i: (bi, 0, k_blk)),
                     pl.BlockSpec((1, n_keys, LANE), lambda bi, i: (bi, 0, v_blk)),
                     pl.BlockSpec((1, n_keys, LANE), lambda bi, i: (bi, 0, ik_blk))]
        args = (k, v, ik)
    return pl.pallas_call(
        body, grid=(b, t // tq), in_specs=in_specs,
        out_specs=pl.BlockSpec((1, tq, B_HEADS * LANE), lambda bi, i: (bi, i, 0)),
        out_shape=jax.ShapeDtypeStruct((b, t, B_HEADS * LANE), BF16),
        scratch_shapes=scratch + _softmax_scratch(rows, tk),
        compiler_params=_params(2), name="dsa_decode" if cached else "dsa_prefill")(q, iq, iw, *args)


def _merge_body(ya_ref, yb_ref, yc_ref, yd_ref, g_ref, wb_ref, o_ref):
    d = o_ref.shape[1]
    acc = 0.0
    for m, y_ref in enumerate((ya_ref, yb_ref, yc_ref, yd_ref)):
        acc = acc + g_ref[:, m * d:(m + 1) * d].astype(F32) * jnp.dot(
            y_ref[...], wb_ref[m], preferred_element_type=F32)
    o_ref[...] = acc.astype(BF16)


def _merge(ys, gate, w_branch):
    m, d = gate.shape[0], w_branch.shape[2]
    tm = _tile(m, 256)
    y_spec = pl.BlockSpec((tm, BR_W), lambda i: (i, 0))
    return pl.pallas_call(
        _merge_body, grid=(m // tm,),
        in_specs=[y_spec] * N_BRANCH + [pl.BlockSpec((tm, N_BRANCH * d), lambda i: (i, 0)),
                                        pl.BlockSpec(w_branch.shape, lambda i: (0, 0, 0))],
        out_specs=pl.BlockSpec((tm, d), lambda i: (i, 0)),
        out_shape=jax.ShapeDtypeStruct((m, d), BF16),
        compiler_params=_params(1), name="merge")(*ys, gate, w_branch)


def _layernorm(z, g, b):
    mu = jnp.mean(z, axis=-1, keepdims=True)
    zc = z - mu
    var = jnp.mean(zc * zc, axis=-1, keepdims=True)
    return zc * lax.rsqrt(var + LN_EPS) * g + b


def _wo_ln_body(m_ref, x_ref, w_ref, g_ref, b_ref, of_ref, ob_ref, *, alpha):
    h = jnp.dot(m_ref[...], w_ref[...], preferred_element_type=F32)
    y = _layernorm(alpha * x_ref[...] + h, g_ref[...], b_ref[...])
    of_ref[...] = y
    ob_ref[...] = y.astype(BF16)


def _wo_ln(merged, x, w_o, g, b, alpha):
    m, d = x.shape
    tm = _tile(m, 256)
    row = pl.BlockSpec((tm, d), lambda i: (i, 0))
    vec = pl.BlockSpec((1, d), lambda i: (0, 0))
    return pl.pallas_call(
        functools.partial(_wo_ln_body, alpha=alpha), grid=(m // tm,),
        in_specs=[row, row, pl.BlockSpec((d, d), lambda i: (0, 0)), vec, vec],
        out_specs=[row, row],
        out_shape=[jax.ShapeDtypeStruct((m, d), F32), jax.ShapeDtypeStruct((m, d), BF16)],
        compiler_params=_params(1), name="wo_ln")(merged, x, w_o, g, b)


def _ffn_body(xb_ref, xf_ref, wu_ref, wd_ref, g_ref, b_ref, of_ref, ob_ref, acc, *, alpha):
    f = pl.program_id(1)

    @pl.when(f == 0)
    def _():
        acc[...] = jnp.zeros_like(acc)

    h = jnp.maximum(jnp.dot(xb_ref[...], wu_ref[...], preferred_element_type=F32), 0.0)
    acc[...] += jnp.dot((h * h).astype(BF16), wd_ref[...], preferred_element_type=F32)

    @pl.when(f == pl.num_programs(1) - 1)
    def _():
        y = _layernorm(alpha * xf_ref[...] + acc[...], g_ref[...], b_ref[...])
        of_ref[...] = y
        ob_ref[...] = y.astype(BF16)


def _ffn(xb, xf, w_up, w_down, g, b, alpha):
    m, d = xf.shape
    dff = w_up.shape[1]
    tm = _tile(m, 512)
    tf = _tile(dff, 512, LANE)
    row = pl.BlockSpec((tm, d), lambda i, f: (i, 0))
    vec = pl.BlockSpec((1, d), lambda i, f: (0, 0))
    return pl.pallas_call(
        functools.partial(_ffn_body, alpha=alpha), grid=(m // tm, dff // tf),
        in_specs=[row, row, pl.BlockSpec((d, tf), lambda i, f: (0, f)),
                  pl.BlockSpec((tf, d), lambda i, f: (f, 0)), vec, vec],
        out_specs=[row, row],
        out_shape=[jax.ShapeDtypeStruct((m, d), F32), jax.ShapeDtypeStruct((m, d), BF16)],
        scratch_shapes=[pltpu.VMEM((tm, d), F32)],
        compiler_params=_params(2), name="ffn")(xb, xf, w_up, w_down, g, b)


def _prep_weights(w_in, w_branch, w_o, w_up, w_down):
    d = w_in.shape[1]
    sizes = (BR_W, BR_W, BR_W, B_HEADS * HD, HD, HD, IDX_HEADS * IDX_HD, IDX_HD, IDX_HEADS,
             C_HEADS * HD, C_HEADS * HD, C_HEADS * HD, C_HEADS,
             D_HEADS * 2 * D_QK, D_HEADS * 2 * D_QK, D_HEADS * D_V, N_BRANCH * d)
    names = ("a_b", "a_c", "a_h", "b_q", "b_k", "b_v", "b_iq", "b_ik", "b_iw",
             "c_q", "c_k", "c_v", "c_f", "d_q", "d_k", "d_v", "gate")
    starts = np.concatenate([[0], np.cumsum(sizes)])
    col = {n: w_in[:, :, int(starts[i]):int(starts[i + 1])] for i, n in enumerate(names)}
    depth = w_in.shape[0]

    def zeros(n):
        return jnp.zeros((depth, d, n), w_in.dtype)

    w_a = jnp.concatenate([col["a_b"], col["a_c"], col["a_h"]], axis=2)
    w_r = jnp.concatenate([col["b_q"], col["b_iq"], col["b_k"], col["b_ik"], zeros(LANE - IDX_HD),
                           col["d_q"], col["d_k"]], axis=2)
    w_p = jnp.concatenate([col["c_q"], col["c_k"], col["c_v"], col["d_v"], col["b_v"],
                           col["b_iw"], col["c_f"], zeros(LANE - IDX_HEADS - C_HEADS)], axis=2)
    assert w_r.shape[2] == R_BLOCKS * LANE and w_p.shape[2] == P_BLOCKS * LANE
    cast = lambda w: w.astype(BF16)
    return dict(w_a=cast(w_a), w_r=cast(w_r), w_p=cast(w_p), w_g=cast(col["gate"]),
                w_branch=cast(w_branch), w_o=cast(w_o), w_up=cast(w_up), w_down=cast(w_down))


def _cols(a, block, n_blocks=1):
    return a[:, :, block * LANE:(block + n_blocks) * LANE]


def _trunk(x, past_len, conv_state, dsa_rows, fox_rows, diff_rows, wts, conv_w, fox_fbias,
           diff_lambda, diff_gain, ln1_g, ln1_b, ln2_g, ln2_b):
    b, t, d = x.shape
    depth = conv_w.shape[0]
    alpha = (2 * depth) ** 0.25
    m = b * t
    n_keys = past_len + t
    pos = jnp.tile(past_len + jnp.arange(t, dtype=jnp.int32), b)
    tab = _rope_tables(pos)
    if past_len:
        fox_cache = jnp.transpose(fox_rows, (0, 1, 4, 3, 2))
        dsa_cache = jnp.transpose(dsa_rows, (0, 1, 3, 2))
        tq_dsa, tk_dsa = t, -(-n_keys // LANE) * LANE
        new_pad = tk_dsa - past_len

        def new_t(a):
            return jnp.pad(jnp.transpose(a, (0, 2, 1)), ((0, 0), (0, 0), (0, new_pad - t))).astype(BF16)
    else:
        tq_att = _tile(t, 512)
        tk_att = _tile(t, 512, LANE)
        tq_dsa, tk_dsa = _tile(t, 128), tk_att

    xf = x.reshape(m, d)
    xb = xf.astype(BF16)
    conv_out, dsa_out, fox_out, diff_out = [], [], [], []
    for l in range(depth):
        lam_init = 0.8 - 0.6 * float(np.exp(-0.3 * l))
        u_a = _matmul("proj_conv", xb, wts["w_a"][l])[0].reshape(b, t, 3 * BR_W)
        rf, rb = _matmul("proj_rope", xb, wts["w_r"][l], tab=tab, kinds=R_KINDS, scales=R_SCALES,
                         want_b16=True)
        pf, pb = _matmul("proj_plain", xb, wts["w_p"][l], kinds=P_KINDS, scales=P_SCALES, want_b16=True)
        (gate,) = _matmul("proj_gate", xb, wts["w_g"][l], sigmoid=True, want_f32=False, want_b16=True,
                          tm_target=512)
        rf, rb, pf, pb = (a.reshape(b, t, a.shape[1]) for a in (rf, rb, pf, pb))

        y_a, n_conv = _short_conv(u_a, conv_state[l], conv_w[l])

        small = _cols(pf, P_SMALL)
        bias_row = jnp.zeros((1, LANE), F32).at[0, CF_LANE:CF_LANE + C_HEADS].set(fox_fbias[l])
        if past_len:
            past_lf = jnp.transpose(fox_cache[l, :, 2 * HD], (0, 2, 1))
            past_lf = jnp.pad(past_lf, ((0, 0), (0, 0), (CF_LANE, LANE - CF_LANE - C_HEADS)))
            small = jnp.concatenate([past_lf, small], axis=1)
        lf, cum = _fox_prep(small, bias_row, past_len)
        lf_new = lf[:, past_len:, CF_LANE:CF_LANE + C_HEADS]
        cum = cum[:, :, CF_LANE:CF_LANE + C_HEADS]
        diff_extra = (diff_lambda[l], diff_gain[l].reshape(1, D_V), jnp.full((1, 1), lam_init, F32))
        if past_len:
            ck = jnp.pad(jnp.transpose(cum, (0, 2, 1)), ((0, 0), (0, 0), (0, past_len + LANE - n_keys)))
            y_c = _decode("fox", l, pb, P_CQ // 4, fox_cache, pb, P_CK // 4, pb, P_CV // 4,
                          (ck[:, :, None, :],), past_len=past_len)
            y_d = _decode("diff", l, rb, R_DQ // 4, diff_rows, rb, R_DK // 4, pb, P_DV // 4,
                          diff_extra, past_len=past_len)
            dsa_kv = (dsa_cache, new_t(_cols(rf, R_BK)), new_t(_cols(pf, P_BV)),
                      new_t(_cols(rf, R_BIK)[..., :IDX_HD]))
        else:
            ck = jnp.transpose(cum, (0, 2, 1)).reshape(b, C_HEADS, t // tk_att, 1, tk_att)
            y_c = _flash("fox", pb, P_CQ, pb, P_CK, pb, P_CV, (ck,), tq=tq_att, tk=tk_att)
            y_d = _flash("diff", rb, R_DQ, rb, R_DK, pb, P_DV, diff_extra, tq=tq_att, tk=tk_att)
            dsa_kv = (rb, R_BK, pb, P_BV, rb, R_BIK)
        y_b = _dsa(rb, rf, pf, dsa_kv, layer=l, past_len=past_len, n_keys=n_keys, tq=tq_dsa, tk=tk_dsa)

        ys = [y.reshape(m, BR_W) for y in (y_a, y_b, y_c, y_d)]
        merged = _merge(ys, gate, wts["w_branch"][l])
        xf, xb = _wo_ln(merged, xf, wts["w_o"][l], ln1_g[l].reshape(1, d), ln1_b[l].reshape(1, d), alpha)
        xf, xb = _ffn(xb, xf, wts["w_up"][l], wts["w_down"][l], ln2_g[l].reshape(1, d),
                      ln2_b[l].reshape(1, d), alpha)

        conv_out.append(n_conv)
        dsa_out.append(jnp.concatenate([_cols(rf, R_BK), _cols(pf, P_BV), _cols(rf, R_BIK)[..., :IDX_HD]],
                                       axis=-1))
        fox_out.append(jnp.concatenate([_cols(pf, P_CK, 4).reshape(b, t, C_HEADS, HD),
                                        _cols(pf, P_CV, 4).reshape(b, t, C_HEADS, HD),
                                        lf_new[..., None]], axis=-1))
        diff_out.append(jnp.concatenate([_cols(rf, R_DK, 4).reshape(b, t, D_HEADS, 2 * D_QK),
                                         _cols(pf, P_DV, 4).reshape(b, t, D_HEADS, D_V)], axis=-1))
    return (xf.reshape(b, t, d), jnp.stack(conv_out), jnp.stack(dsa_out), jnp.stack(fox_out),
            jnp.stack(diff_out))


def kernel(x_prompt, x_sample, state_conv, cache_dsa, cache_fox, cache_diff, w_in, conv_w, fox_fbias,
           diff_lambda, diff_gain, w_branch, w_o, ln1_g, ln1_b, w_up, w_down, ln2_g, ln2_b):
    depth = w_in.shape[0]
    b = x_prompt.shape[0]
    wts = _prep_weights(w_in, w_branch, w_o, w_up, w_down)
    shared = (wts, conv_w, fox_fbias, diff_lambda, diff_gain, ln1_g, ln1_b, ln2_g, ln2_b)
    conv0 = jnp.zeros((depth, b, CONV_W - 1, BR_W), x_prompt.dtype)
    y_p, conv_p, dsa_p, fox_p, diff_p = _trunk(x_prompt, 0, conv0, None, None, None, *shared)
    y_s, conv_s, dsa_s, fox_s, diff_s = _trunk(x_sample, cache_dsa.shape[2], state_conv, cache_dsa,
                                               cache_fox, cache_diff, *shared)
    return (y_p, y_s, conv_p, dsa_p, fox_p, diff_p, conv_s, dsa_s, fox_s, diff_s)
```

```python
import functools

import numpy as np
import jax
import jax.numpy as jnp
from jax import lax
from jax.experimental import pallas as pl
from jax.experimental.pallas import tpu as pltpu

CHUNK = 64
N_BRANCH = 4
BR_W = 512
HD = 128
CONV_W = 3
B_HEADS = 4
IDX_HEADS = 4
IDX_HD = 64
TOPK_MAX = 256
IDX_SCALE = (IDX_HEADS * IDX_HD) ** -0.5
C_HEADS = 4
D_HEADS = 4
D_QK = 64
D_V = 128
ROPE_THETA = 500000.0
LN_EPS = 1e-5
RMS_EPS = 1e-5

LANE = 128
STRIP = 16
VMEM_LIMIT = 56 * 1024 * 1024
NEG = -1e30
LOG2E = 1.4426950408889634
INT_MIN = np.int32(-2 ** 31)
F32 = jnp.float32
BF16 = jnp.bfloat16

R_BQ, R_BIQ, R_BK, R_BIK, R_DQ, R_DK, R_BLOCKS = 0, 4, 6, 7, 8, 12, 16
P_CQ, P_CK, P_CV, P_DV, P_BV, P_SMALL, P_BLOCKS = 0, 4, 8, 12, 16, 17, 18
IW_LANE, CF_LANE = 0, IDX_HEADS
R_KINDS = ("r128",) * 4 + ("r64",) * 2 + ("r128", "r64") + ("r64",) * 8
R_SCALES = (HD ** -0.5 * LOG2E,) * 4 + (1.0,) * 4 + (D_QK ** -0.5 * LOG2E,) * 4 + (1.0,) * 4
P_KINDS = (None,) * P_BLOCKS
P_SCALES = (HD ** -0.5 * LOG2E,) * 4 + (1.0,) * 14


def _tile(n, target, align=8):
    if n <= target:
        return n
    for t in range(target - target % align, 0, -align):
        if n % t == 0:
            return t
    return n


def _params(n_grid):
    return pltpu.CompilerParams(dimension_semantics=("arbitrary",) * n_grid,
                                vmem_limit_bytes=VMEM_LIMIT)


def _mm_body(x_ref, w_ref, *rest, kinds, scales, has_tab, sigmoid, want_f32, want_b16):
    tab_ref = rest[0] if has_tab else None
    outs = rest[1:] if has_tab else rest
    of_ref = outs[0] if want_f32 else None
    ob_ref = outs[-1] if want_b16 else None
    acc = jnp.dot(x_ref[...], w_ref[...], preferred_element_type=F32)
    if kinds is None:
        if sigmoid:
            acc = jax.nn.sigmoid(acc)
        if want_f32:
            of_ref[...] = acc
        if want_b16:
            ob_ref[...] = acc.astype(BF16)
        return
    for c, (kind, scale) in enumerate(zip(kinds, scales)):
        sl = slice(c * LANE, (c + 1) * LANE)
        blk = acc[:, sl]
        if kind is not None:
            base = 0 if kind == "r128" else 3 * LANE
            half = (HD // 4) // 2 if kind == "r128" else (IDX_HD // 4) // 2
            cos = tab_ref[:, base:base + LANE]
            s_lo = tab_ref[:, base + LANE:base + 2 * LANE]
            s_hi = tab_ref[:, base + 2 * LANE:base + 3 * LANE]
            blk = (blk * cos + pltpu.roll(blk, LANE - half, 1) * s_lo
                   + pltpu.roll(blk, half, 1) * s_hi)
        if want_f32:
            of_ref[:, sl] = blk
        if want_b16:
            ob_ref[:, sl] = (blk * scale if scale != 1.0 else blk).astype(BF16)


def _matmul(name, x, w, *, tab=None, kinds=None, scales=None, sigmoid=False,
            want_f32=True, want_b16=False, tm_target=256, tn_target=2048):
    m, k = x.shape
    n = w.shape[1]
    tm = _tile(m, tm_target)
    tn = n if kinds is not None else _tile(n, tn_target, LANE)
    grid = (n // tn, m // tm)
    in_specs = [pl.BlockSpec((tm, k), lambda j, i: (i, 0)),
                pl.BlockSpec((k, tn), lambda j, i: (0, j))]
    args = [x, w]
    if tab is not None:
        in_specs.append(pl.BlockSpec((tm, tab.shape[1]), lambda j, i: (i, 0)))
        args.append(tab)
    out_shape, out_specs = [], []
    if want_f32:
        out_shape.append(jax.ShapeDtypeStruct((m, n), F32))
        out_specs.append(pl.BlockSpec((tm, tn), lambda j, i: (i, j)))
    if want_b16:
        out_shape.append(jax.ShapeDtypeStruct((m, n), BF16))
        out_specs.append(pl.BlockSpec((tm, tn), lambda j, i: (i, j)))
    body = functools.partial(_mm_body, kinds=kinds, scales=scales, has_tab=tab is not None,
                             sigmoid=sigmoid, want_f32=want_f32, want_b16=want_b16)
    return pl.pallas_call(body, grid=grid, in_specs=in_specs, out_specs=out_specs,
                          out_shape=out_shape, compiler_params=_params(2), name=name)(*args)


def _rope_tables(pos):
    tabs = []
    for hd in (HD, IDX_HD):
        rot = hd // 4
        half = rot // 2
        freqs = ROPE_THETA ** (-2.0 * jnp.arange(half, dtype=F32) / rot)
        ang = pos.astype(F32)[:, None] * freqs[None, :]
        cos, sin = jnp.cos(ang), jnp.sin(ang)
        m = pos.shape[0]
        one = jnp.ones((m, hd - rot), F32)
        zero_h = jnp.zeros((m, half), F32)
        zero_t = jnp.zeros((m, hd - rot), F32)
        c = jnp.concatenate([cos, cos, one], axis=1)
        s_lo = jnp.concatenate([-sin, zero_h, zero_t], axis=1)
        s_hi = jnp.concatenate([zero_h, sin, zero_t], axis=1)
        rep = LANE // hd
        tabs += [jnp.tile(c, (1, rep)), jnp.tile(s_lo, (1, rep)), jnp.tile(s_hi, (1, rep))]
    return jnp.concatenate(tabs, axis=1)


def _conv_body(u_ref, st_ref, w_ref, y_ref, ns_ref, carry, *, tm):
    @pl.when(pl.program_id(1) == 0)
    def _():
        carry[...] = st_ref[0]

    gate_b = u_ref[0, :, 0:BR_W]
    v = u_ref[0, :, BR_W:2 * BR_W] * u_ref[0, :, 2 * BR_W:3 * BR_W]
    p0 = carry[0:1, :]
    p1 = carry[1:2, :]
    row = lax.broadcasted_iota(jnp.int32, (tm, BR_W), 0)
    v1 = jnp.where(row == 0, p1, pltpu.roll(v, 1, 0))
    v2 = jnp.where(row == 0, p0, jnp.where(row == 1, p1, pltpu.roll(v, 2, 0)))
    w = w_ref[...]
    y = gate_b * (v2 * w[0:1, :] + v1 * w[1:2, :] + v * w[2:3, :])
    y_ref[0] = y.astype(BF16)
    tail = (u_ref[0, tm - 2:tm, BR_W:2 * BR_W] * u_ref[0, tm - 2:tm, 2 * BR_W:3 * BR_W])
    carry[...] = tail
    ns_ref[0] = tail


def _short_conv(u_a, state, conv_w):
    b, t, _ = u_a.shape
    assert t >= CONV_W - 1
    tm = _tile(t, 512)
    body = functools.partial(_conv_body, tm=tm)
    return pl.pallas_call(
        body, grid=(b, t // tm),
        in_specs=[pl.BlockSpec((1, tm, 3 * BR_W), lambda bi, i: (bi, i, 0)),
                  pl.BlockSpec((1, CONV_W - 1, BR_W), lambda bi, i: (bi, 0, 0)),
                  pl.BlockSpec((CONV_W, BR_W), lambda bi, i: (0, 0))],
        out_specs=[pl.BlockSpec((1, tm, BR_W), lambda bi, i: (bi, i, 0)),
                   pl.BlockSpec((1, CONV_W - 1, BR_W), lambda bi, i: (bi, 0, 0))],
        out_shape=[jax.ShapeDtypeStruct((b, t, BR_W), BF16),
                   jax.ShapeDtypeStruct((b, CONV_W - 1, BR_W), F32)],
        scratch_shapes=[pltpu.VMEM((CONV_W - 1, BR_W), F32)],
        compiler_params=_params(2), name="short_conv")(u_a, state, conv_w)


def _fox_prep_body(x_ref, bias_ref, lf_ref, cum_ref, carry, *, tl, past_len):
    i = pl.program_id(1)

    @pl.when(i == 0)
    def _():
        carry[...] = jnp.zeros_like(carry)

    x = x_ref[0]
    z = x + bias_ref[...]
    log_sig = jnp.minimum(z, 0.0) - jnp.log1p(jnp.exp(-jnp.abs(z)))
    row = i * tl + lax.broadcasted_iota(jnp.int32, (tl, 1), 0)
    lf = jnp.where(row >= past_len, log_sig, x)
    r = lax.broadcasted_iota(jnp.int32, (tl, tl), 0)
    c = lax.broadcasted_iota(jnp.int32, (tl, tl), 1)
    tri = jnp.where(c <= r, 1.0, 0.0).astype(F32)
    cs = jnp.dot(tri, lf, precision=lax.Precision.HIGHEST, preferred_element_type=F32) + carry[...]
    lf_ref[0] = lf
    cum_ref[0] = cs
    carry[...] = cs[tl - 1:tl, :]


def _fox_prep(x, bias_row, past_len):
    b, l, _ = x.shape
    tl = _tile(l, 512)
    body = functools.partial(_fox_prep_body, tl=tl, past_len=past_len)
    spec = pl.BlockSpec((1, tl, LANE), lambda bi, i: (bi, i, 0))
    return pl.pallas_call(
        body, grid=(b, l // tl),
        in_specs=[spec, pl.BlockSpec((1, LANE), lambda bi, i: (0, 0))],
        out_specs=[spec, spec],
        out_shape=[jax.ShapeDtypeStruct((b, l, LANE), F32)] * 2,
        scratch_shapes=[pltpu.VMEM((1, LANE), F32)],
        compiler_params=_params(2), name="fox_prep")(x, bias_row)


def _row_limits(qpos, chunk):
    if chunk == 1:
        return qpos + 1
    sh = chunk.bit_length() - 1
    return lax.shift_left(lax.shift_right_arithmetic(qpos, sh) + 1, sh)


def _tile_counts(q0, tq, tk, chunk, n_keys):
    lim_min = (q0 // chunk + 1) * chunk
    lim_max = ((q0 + tq - 1) // chunk + 1) * chunk
    n_full = jnp.minimum(lim_min, n_keys) // tk
    n_need = (jnp.minimum(lim_max, n_keys) + tk - 1) // tk
    return n_full, n_need


def _softmax_scratch(rows, tk):
    return [pltpu.VMEM((rows, LANE), F32)] * 4 + [pltpu.VMEM((2, rows, tk), BF16)]


def _softmax_init(m_sc, l_sc, acc_sc):
    m_sc[...] = jnp.full(m_sc.shape, NEG, F32)
    l_sc[...] = jnp.zeros(l_sc.shape, F32)
    acc_sc[...] = jnp.zeros(acc_sc.shape, F32)


def _softmax_tile(s, pv, state, fix=None, base=0, slot=0):
    m_sc, l_sc, a_sc, acc_sc, p_sc = state
    rows, tk = s.shape
    for r0 in range(0, rows, STRIP):
        rs = slice(base + r0, base + r0 + STRIP)
        x = s[r0:r0 + STRIP, :]
        if fix is not None:
            x = fix(x, r0)
        blocks = [x[:, c * LANE:(c + 1) * LANE] for c in range(tk // LANE)]
        mx = blocks[0]
        for blk in blocks[1:]:
            mx = jnp.maximum(mx, blk)
        m_old = m_sc[rs, :]
        m_new = jnp.maximum(m_old, jnp.max(mx, axis=1, keepdims=True))
        alpha = jnp.exp2(m_old - m_new)
        tot = None
        for c, blk in enumerate(blocks):
            p = jnp.exp2(blk - m_new)
            tot = p if tot is None else tot + p
            p_sc[slot, rs, c * LANE:(c + 1) * LANE] = p.astype(BF16)
        l_sc[rs, :] = alpha * l_sc[rs, :] + tot
        m_sc[rs, :] = m_new
        a_sc[rs, :] = alpha
    rs = slice(base, base + rows)
    acc_sc[rs, :] = a_sc[rs, :] * acc_sc[rs, :] + pv(p_sc[slot, rs, 0:tk])


def _pv(v):
    return lambda p: jnp.dot(p, v, preferred_element_type=F32)


def _softmax_out(state):
    _, l_sc, _, acc_sc, _ = state
    return acc_sc[...] / jnp.sum(l_sc[...], axis=1, keepdims=True)


def _qk(q, k):
    return lax.dot_general(q, k, (((1,), (1,)), ((), ())), preferred_element_type=F32)


def _split_subheads(q):
    qf = q.astype(F32)
    lane = lax.broadcasted_iota(jnp.int32, qf.shape, 1)
    return jnp.concatenate([jnp.where(lane < D_QK, qf, 0.0), jnp.where(lane >= D_QK, qf, 0.0)],
                           axis=0).astype(BF16)


def _diff_out(o, tq, lam_ref, gain_ref, li_ref):
    lam = lam_ref[...]
    li = li_ref[...]
    lmb = (jnp.exp(jnp.sum(lam[0:1, :] * lam[1:2, :], axis=1, keepdims=True))
           - jnp.exp(jnp.sum(lam[2:3, :] * lam[3:4, :], axis=1, keepdims=True)) + li)
    o = o[0:tq, :] - lmb * o[tq:2 * tq, :]
    o = o * lax.rsqrt(jnp.mean(o * o, axis=1, keepdims=True) + RMS_EPS)
    return o * gain_ref[...] * (1.0 - li)


def _flash_body(*refs, tq, tk, chunk, mode):
    if mode == "fox":
        q_ref, k_ref, v_ref, ck_ref = refs[:4]
        o_ref, state = refs[4], refs[5:]
    else:
        q_ref, k_ref, v_ref, lam_ref, gain_ref, li_ref = refs[:6]
        o_ref, state = refs[6], refs[7:]
    n_keys = k_ref.shape[1]
    q0 = pl.program_id(2) * tq
    n_full, n_need = _tile_counts(q0, tq, tk, chunk, n_keys)
    qlim = _row_limits(q0 + lax.broadcasted_iota(jnp.int32, (tq, 1), 0), chunk)
    if mode == "fox":
        q = q_ref[0]
    else:
        q = _split_subheads(q_ref[0])
        qlim = jnp.concatenate([qlim, qlim], axis=0)
    _softmax_init(state[0], state[1], state[3])

    def step(j, masked, slot):
        ks = pl.multiple_of(j * tk, tk)
        s = _qk(q, k_ref[0, pl.ds(ks, tk), :])
        ck = ck_ref[0, 0, j] * LOG2E if mode == "fox" else None
        kpos = ks + lax.broadcasted_iota(jnp.int32, (1, tk), 1)

        def fix(x, r0):
            if ck is not None:
                x = x - ck
            if masked:
                x = jnp.where(kpos < qlim[r0:r0 + STRIP, :], x, NEG)
            return x

        _softmax_tile(s, _pv(v_ref[0, pl.ds(ks, tk), :]), state,
                      fix if (masked or ck is not None) else None, slot=slot)

    def full_pair(jj, c):
        step(2 * jj, False, 0)
        step(2 * jj + 1, False, 1)
        return c

    def masked_step(j, c):
        step(j, True, 0)
        return c

    lax.fori_loop(0, n_full // 2, full_pair, 0)

    @pl.when(n_full % 2 == 1)
    def _():
        step(n_full - 1, False, 0)

    lax.fori_loop(n_full, n_need, masked_step, 0)

    o = _softmax_out(state)
    if mode == "diff":
        o = _diff_out(o, tq, lam_ref, gain_ref, li_ref)
    o_ref[0] = o.astype(BF16)


def _flash(mode, q, q_off, k, k_off, v, v_off, extra, *, tq, tk):
    b, t, _ = q.shape
    heads = C_HEADS if mode == "fox" else D_HEADS
    stack = 1 if mode == "fox" else 2
    chunk = 1 if mode == "fox" else CHUNK
    in_specs = [pl.BlockSpec((1, tq, LANE), lambda bi, h, i: (bi, i, q_off + h)),
                pl.BlockSpec((1, t, LANE), lambda bi, h, i: (bi, 0, k_off + h)),
                pl.BlockSpec((1, t, LANE), lambda bi, h, i: (bi, 0, v_off + h))]
    if mode == "fox":
        in_specs.append(pl.BlockSpec((1, 1, t // tk, 1, tk), lambda bi, h, i: (bi, h, 0, 0, 0)))
    else:
        in_specs += [pl.BlockSpec(e.shape, lambda bi, h, i: (0, 0)) for e in extra]
    body = functools.partial(_flash_body, tq=tq, tk=tk, chunk=chunk, mode=mode)
    return pl.pallas_call(
        body, grid=(b, heads, t // tq), in_specs=in_specs,
        out_specs=pl.BlockSpec((1, tq, LANE), lambda bi, h, i: (bi, i, h)),
        out_shape=jax.ShapeDtypeStruct((b, t, heads * LANE), BF16),
        scratch_shapes=_softmax_scratch(stack * tq, tk),
        compiler_params=_params(3), name=mode + "_prefill")(q, k, v, *extra)


def _decode_body(*refs, t, past_len, mode):
    if mode == "fox":
        q_ref, cache_ref, kn_ref, vn_ref, ck_ref = refs[:5]
        o_ref, state = refs[5], refs[6:]
    else:
        q_ref, cache_ref, kn_ref, vn_ref, lam_ref, gain_ref, li_ref = refs[:7]
        o_ref, state = refs[7], refs[8:]
    heads = C_HEADS
    stack = 1 if mode == "fox" else 2
    rows = stack * t
    tt = lax.broadcasted_iota(jnp.int32, (t, 1), 0)
    qlim = jnp.minimum(_row_limits(past_len + tt, 1 if mode == "fox" else CHUNK), past_len + t)
    qlim = jnp.concatenate([qlim] * stack, axis=0)
    new_pos = past_len + lax.broadcasted_iota(jnp.int32, (1, LANE), 1)
    zpad = jnp.zeros((LANE - t, LANE), BF16)
    _softmax_init(state[0], state[1], state[3])

    for h in range(heads):
        hs = slice(h * LANE, (h + 1) * LANE)
        base = h * rows
        qh = q_ref[0, :, hs]
        if mode == "fox":
            kt = cache_ref[0, 0, 0:HD, h, :].astype(BF16)
            vt = cache_ref[0, 0, HD:2 * HD, h, :].astype(BF16)
            ck_past = ck_ref[0, h, :, 0:past_len] * LOG2E
            ck_new = ck_ref[0, h, :, past_len:past_len + LANE] * LOG2E
            s = jnp.dot(qh, kt, preferred_element_type=F32)
            _softmax_tile(s, lambda p, vt=vt: _qk(p, vt), state,
                          lambda x, r0, ck_past=ck_past: x - ck_past, base)
        else:
            qh = _split_subheads(qh)
            k = cache_ref[0, 0, :, h, 0:2 * D_QK].astype(BF16)
            v = cache_ref[0, 0, :, h, 2 * D_QK:].astype(BF16)
            ck_new = None
            _softmax_tile(_qk(qh, k), _pv(v), state, None, base)

        def fix(x, r0, ck_new=ck_new):
            if ck_new is not None:
                x = x - ck_new
            return jnp.where(new_pos < qlim[r0:r0 + STRIP, :], x, NEG)

        kn = jnp.concatenate([kn_ref[0, :, hs], zpad], axis=0)
        vn = jnp.concatenate([vn_ref[0, :, hs], zpad], axis=0)
        _softmax_tile(_qk(qh, kn), _pv(vn), state, fix, base)

    o = _softmax_out(state)
    for h in range(heads):
        oh = o[h * rows:(h + 1) * rows, :]
        if mode == "diff":
            oh = _diff_out(oh, t, lam_ref, gain_ref, li_ref)
        o_ref[0, :, h * LANE:(h + 1) * LANE] = oh.astype(BF16)


def _decode(mode, layer, q, q_blk, cache, k_new, k_blk, v_new, v_blk, extra, *, past_len):
    b, t, _ = q.shape
    heads = C_HEADS
    assert t <= LANE and t % STRIP == 0 and past_len % LANE == 0
    stack = 1 if mode == "fox" else 2
    wide = lambda blk: pl.BlockSpec((1, t, heads * LANE), lambda bi: (bi, 0, blk))
    in_specs = [wide(q_blk), pl.BlockSpec((1, 1) + cache.shape[2:], lambda bi: (layer, bi, 0, 0, 0)),
                wide(k_blk), wide(v_blk)]
    if mode == "fox":
        in_specs.append(pl.BlockSpec((1,) + extra[0].shape[1:], lambda bi: (bi, 0, 0, 0)))
    else:
        in_specs += [pl.BlockSpec(e.shape, lambda bi: (0, 0)) for e in extra]
    body = functools.partial(_decode_body, t=t, past_len=past_len, mode=mode)
    return pl.pallas_call(
        body, grid=(b,), in_specs=in_specs, out_specs=wide(0),
        out_shape=jax.ShapeDtypeStruct((b, t, heads * LANE), BF16),
        scratch_shapes=_softmax_scratch(stack * heads * t, past_len),
        compiler_params=_params(1), name=mode + "_decode")(q, cache, k_new, v_new, *extra)


def _dsa_body(*refs, tq, tk, past_len, n_keys, k_sel, cached):
    if cached:
        (q_ref, iq_ref, iw_ref, cache_ref, knt_ref, vnt_ref, iknt_ref, o_ref,
         keys, keys_hi, bias_sc, kt_all, vt_all, ikt_all) = refs[:14]
        state = refs[14:]
        for dst, new, lo, hi in ((kt_all, knt_ref, 0, HD), (vt_all, vnt_ref, HD, 2 * HD),
                                 (ikt_all, iknt_ref, 2 * HD, 2 * HD + IDX_HD)):
            dst[:, 0:past_len] = cache_ref[0, 0, lo:hi, :].astype(BF16)
            dst[:, past_len:] = new[0]
        index_logits = lambda iq, ks: jnp.dot(iq[:, 0:IDX_HD], ikt_all[...], preferred_element_type=F32)
        key_logits = lambda q, ks: jnp.dot(q, kt_all[...], preferred_element_type=F32)
        values = lambda ks: (lambda p: _qk(p, vt_all[...]))
    else:
        q_ref, iq_ref, iw_ref, k_ref, v_ref, ik_ref, o_ref, keys, keys_hi, bias_sc = refs[:10]
        state = refs[10:]
        index_logits = lambda iq, ks: _qk(iq, ik_ref[0, pl.ds(ks, tk), :])
        key_logits = lambda q, ks: _qk(q, k_ref[0, pl.ds(ks, tk), :])
        values = lambda ks: _pv(v_ref[0, pl.ds(ks, tk), :])

    q0 = past_len + pl.program_id(1) * tq
    _, n_need = _tile_counts(q0, tq, tk, CHUNK, n_keys)
    qlim = jnp.minimum(_row_limits(q0 + lax.broadcasted_iota(jnp.int32, (tq, 1), 0), CHUNK), n_keys)

    iqf = iq_ref[0]
    parts = []
    for h in range(IDX_HEADS):
        blk = iqf[:, (h // 2) * LANE:(h // 2 + 1) * LANE]
        parts.append(pltpu.roll(blk, IDX_HD, 1) if h % 2 else blk)
    iq = jnp.concatenate([p[r0:r0 + STRIP, :] for r0 in range(0, tq, STRIP) for p in parts],
                         axis=0).astype(BF16)
    iw = iw_ref[0]
    fold_scale = np.frexp(IDX_SCALE)[0] == 0.5
    w_b = [jnp.broadcast_to(iw[:, IW_LANE + h:IW_LANE + h + 1] * (IDX_SCALE if fold_scale else 1.0),
                            (tq, LANE)) for h in range(IDX_HEADS)]
    qlim_b = jnp.broadcast_to(qlim, (tq, LANE))
    lane = lax.broadcasted_iota(jnp.int32, (1, LANE), 1)

    def score_tile(j, c):
        ks = pl.multiple_of(j * tk, tk)
        rel = index_logits(iq, ks)
        for r0 in range(0, tq, STRIP):
            rs = slice(r0, r0 + STRIP)
            for c0 in range(0, tk, LANE):
                cs = slice(c0, c0 + LANE)
                sc = None
                for h in range(IDX_HEADS):
                    first = (r0 * IDX_HEADS) + h * STRIP
                    term = jnp.maximum(rel[first:first + STRIP, cs], 0.0) * w_b[h][rs, :]
                    sc = term if sc is None else sc + term
                if not fold_scale:
                    sc = sc * IDX_SCALE
                sc = jnp.where(sc == 0.0, 0.0, sc)
                bits = lax.bitcast_convert_type(sc, jnp.int32)
                key = jnp.where(bits < 0, bits ^ jnp.int32(0x7FFFFFFF), bits)
                key = jnp.where(ks + c0 + lane < qlim_b[rs, :], key, INT_MIN)
                keys[j, rs, cs] = key
                keys_hi[j, rs, cs] = lax.shift_right_arithmetic(key, 16).astype(jnp.int16)
        return c

    lax.fori_loop(0, n_need, score_tile, 0)

    def count(pred):
        def add(j, acc):
            hit = jnp.where(pred(keys[j]), 1.0, 0.0)
            for c in range(tk // LANE):
                acc = acc + hit[:, c * LANE:(c + 1) * LANE]
            return acc

        def pair(jj, acc):
            return add(2 * jj + 1, add(2 * jj, acc))

        acc = lax.fori_loop(0, n_need // 2, pair, jnp.zeros((tq, LANE), F32))
        acc = lax.cond(n_need % 2 == 1, lambda a: add(n_need - 1, a), lambda a: a, acc)
        return jnp.sum(acc, axis=1, keepdims=True)

    def count_ge(cand):
        return count(lambda t: t >= cand)

    def count_hi_ge(cand):
        cand = jnp.broadcast_to(cand, (tq, LANE)).astype(jnp.int16)
        one = jnp.ones((tq, LANE), BF16)

        def add(j, acc):
            t = keys_hi[j]
            for c in range(tk // LANE):
                acc = acc + jnp.where(t[:, c * LANE:(c + 1) * LANE] >= cand, one, jnp.zeros_like(one))
            return acc

        acc = lax.fori_loop(0, n_need, add, jnp.zeros((tq, LANE), BF16))
        return jnp.sum(acc.astype(F32), axis=1, keepdims=True)

    def any_row(flag):
        return (jnp.max(jnp.where(flag, 1.0, 0.0)) > 0.0).astype(jnp.int32)

    want = jnp.float32(k_sel)
    crowded = count_ge(jnp.full((tq, 1), INT_MIN + 1, jnp.int32)) > want

    zero = jnp.zeros((tq, 1), jnp.int32)
    n_nonneg = count_ge(zero)
    n_pos = count_ge(zero + 1)
    at_zero = crowded & (n_pos < want) & (n_nonneg >= want)
    searching = crowded & jnp.logical_not(at_zero)
    half = 2 ** 15

    def upper_bit(b, st):
        ans, n_ans = st
        cand = ans + lax.shift_left(jnp.int32(1), 15 - b)
        n = count_hi_ge(cand)
        take = (n >= want) & searching
        return jnp.where(take, cand, ans), jnp.where(take, n, n_ans)

    top, n_top = lax.fori_loop(0, 16, upper_bit, (jnp.full((tq, 1), -half, jnp.int32),
                                                  jnp.full((tq, 1), 2.0 * k_sel + 1.0, F32)))
    go = any_row(searching & (n_top != want))
    n_above = jnp.where(top < half - 1, count_hi_ge(jnp.minimum(top + 1, half - 1)), 0.0)

    @pl.when(go > 0)
    def _():
        top_b = jnp.broadcast_to(top, (tq, LANE)).astype(jnp.int16)
        floor = jnp.full((tq, LANE), -half, jnp.int16)

        def keep_bucket(j, c):
            low = ((keys[j] & jnp.int32(0xFFFF)) - half).astype(jnp.int16)
            upper = keys_hi[j]
            for c0 in range(0, tk, LANE):
                cs = slice(c0, c0 + LANE)
                keys_hi[j, :, cs] = jnp.where(upper[:, cs] == top_b, low[:, cs], floor)
            return c

        lax.fori_loop(0, n_need, keep_bucket, 0)

    def lower_bit(st):
        b, _, ans, n_ans = st
        cand = ans + lax.shift_left(jnp.int32(1), 15 - b)
        n = n_above + count_hi_ge(cand - half)
        take = (n >= want) & searching
        ans = jnp.where(take, cand, ans)
        n_ans = jnp.where(take, n, n_ans)
        return b + 1, any_row(searching & (n_ans != want)), ans, n_ans

    _, _, low, n_low = lax.while_loop(lambda st: (st[0] < 16) & (st[1] > 0), lower_bit,
                                      (jnp.int32(0), go, zero, n_top))
    kstar = jnp.where(at_zero, zero, lax.shift_left(top, 16) + low)
    n_ge = jnp.where(at_zero, n_nonneg, n_low)
    tie = crowded & (n_ge > want)

    @pl.when(any_row(tie) > 0)
    def _():
        n_gt = lax.cond(any_row(tie & jnp.logical_not(at_zero)) > 0, lambda: count_ge(kstar + 1),
                        lambda: jnp.zeros((tq, 1), F32))
        need = want - jnp.where(at_zero, n_pos, n_gt)
        pw = next(w for w in (512, 256, LANE) if tk % w == 0)
        r = lax.broadcasted_iota(jnp.int32, (pw, pw), 0)
        c = lax.broadcasted_iota(jnp.int32, (pw, pw), 1)
        prefix = jnp.where(r <= c, 1.0, 0.0).astype(BF16)

        def drop(j, seen):
            t = keys[j]
            eq = tie & (t == kstar)
            kept = []
            for c0 in range(0, tk, pw):
                eq_c = eq[:, c0:c0 + pw]
                rank = seen + jnp.dot(jnp.where(eq_c, 1.0, 0.0).astype(BF16), prefix,
                                      preferred_element_type=F32)
                kept.append(jnp.where(eq_c & (rank > need), INT_MIN, t[:, c0:c0 + pw]))
                seen = rank[:, pw - 1:pw]
            keys[j] = kept[0] if len(kept) == 1 else jnp.concatenate(kept, axis=1)
            return seen

        lax.fori_loop(0, n_need, drop, jnp.zeros((tq, 1), F32))

    thr = jnp.maximum(kstar, INT_MIN + 1)
    qf = q_ref[0]
    q = jnp.concatenate([qf[:, h * LANE:(h + 1) * LANE] for h in range(B_HEADS)], axis=0)
    _softmax_init(state[0], state[1], state[3])

    def attend(j, slot):
        ks = pl.multiple_of(j * tk, tk)
        bias_sc[slot] = jnp.where(keys[j] >= thr, 0.0, NEG)
        _softmax_tile(key_logits(q, ks), values(ks), state,
                      lambda x, r0: x + bias_sc[slot, r0 % tq:r0 % tq + STRIP, :], slot=slot)

    def attend_pair(jj, c):
        attend(2 * jj, 0)
        attend(2 * jj + 1, 1)
        return c

    lax.fori_loop(0, n_need // 2, attend_pair, 0)

    @pl.when(n_need % 2 == 1)
    def _():
        attend(n_need - 1, 0)

    o = _softmax_out(state)
    for h in range(B_HEADS):
        o_ref[0, :, h * LANE:(h + 1) * LANE] = o[h * tq:(h + 1) * tq, :].astype(BF16)


def _dsa(q, iq, iw, kv, *, layer, past_len, n_keys, tq, tk):
    b, t, _ = q.shape
    cached = past_len > 0
    lpad = -(-n_keys // tk) * tk
    assert not cached or (lpad == tk and past_len % LANE == 0)
    k_sel = min(TOPK_MAX, n_keys // 4)
    body = functools.partial(_dsa_body, tq=tq, tk=tk, past_len=past_len, n_keys=n_keys, k_sel=k_sel,
                             cached=cached)
    rows = B_HEADS * tq
    in_specs = [pl.BlockSpec((1, tq, B_HEADS * LANE), lambda bi, i: (bi, i, R_BQ // B_HEADS)),
                pl.BlockSpec((1, tq, 2 * LANE), lambda bi, i: (bi, i, R_BIQ // 2)),
                pl.BlockSpec((1, tq, LANE), lambda bi, i: (bi, i, P_SMALL))]
    assert lpad // LANE <= 256
    scratch = [pltpu.VMEM((lpad // tk, tq, tk), jnp.int32), pltpu.VMEM((lpad // tk, tq, tk), jnp.int16),
               pltpu.VMEM((2, tq, tk), F32)]
    if cached:
        cache = kv[0]
        in_specs.append(pl.BlockSpec((1, 1) + cache.shape[2:], lambda bi, i: (layer, bi, 0, 0)))
        in_specs += [pl.BlockSpec((1,) + a.shape[1:], lambda bi, i: (bi, 0, 0)) for a in kv[1:]]
        args = kv
        scratch += [pltpu.VMEM((a.shape[1], lpad), BF16) for a in kv[1:]]
    else:
        k, k_blk, v, v_blk, ik, ik_blk = kv
        in_specs += [pl.BlockSpec((1, n_keys, LANE), lambda bi, i: (bi, 0, k_blk)),
                     pl.BlockSpec((1, n_keys, LANE), lambda bi, i: (bi, 0, v_blk)),
                     pl.BlockSpec((1, n_keys, LANE), lambda bi, i: (bi, 0, ik_blk))]
        args = (k, v, ik)
    return pl.pallas_call(
        body, grid=(b, t // tq), in_specs=in_specs,
        out_specs=pl.BlockSpec((1, tq, B_HEADS * LANE), lambda bi, i: (bi, i, 0)),
        out_shape=jax.ShapeDtypeStruct((b, t, B_HEADS * LANE), BF16),
        scratch_shapes=scratch + _softmax_scratch(rows, tk),
        compiler_params=_params(2), name="dsa_decode" if cached else "dsa_prefill")(q, iq, iw, *args)


def _merge_body(ya_ref, yb_ref, yc_ref, yd_ref, g_ref, wb_ref, o_ref):
    d = o_ref.shape[1]
    acc = 0.0
    for m, y_ref in enumerate((ya_ref, yb_ref, yc_ref, yd_ref)):
        acc = acc + g_ref[:, m * d:(m + 1) * d].astype(F32) * jnp.dot(
            y_ref[...], wb_ref[m], preferred_element_type=F32)
    o_ref[...] = acc.astype(BF16)


def _merge(ys, gate, w_branch):
    m, d = gate.shape[0], w_branch.shape[2]
    tm = _tile(m, 256)
    y_spec = pl.BlockSpec((tm, BR_W), lambda i: (i, 0))
    return pl.pallas_call(
        _merge_body, grid=(m // tm,),
        in_specs=[y_spec] * N_BRANCH + [pl.BlockSpec((tm, N_BRANCH * d), lambda i: (i, 0)),
                                        pl.BlockSpec(w_branch.shape, lambda i: (0, 0, 0))],
        out_specs=pl.BlockSpec((tm, d), lambda i: (i, 0)),
        out_shape=jax.ShapeDtypeStruct((m, d), BF16),
        compiler_params=_params(1), name="merge")(*ys, gate, w_branch)


def _layernorm(z, g, b):
    mu = jnp.mean(z, axis=-1, keepdims=True)
    zc = z - mu
    var = jnp.mean(zc * zc, axis=-1, keepdims=True)
    return zc * lax.rsqrt(var + LN_EPS) * g + b


def _wo_ln_body(m_ref, x_ref, w_ref, g_ref, b_ref, of_ref, ob_ref, *, alpha):
    h = jnp.dot(m_ref[...], w_ref[...], preferred_element_type=F32)
    y = _layernorm(alpha * x_ref[...] + h, g_ref[...], b_ref[...])
    of_ref[...] = y
    ob_ref[...] = y.astype(BF16)


def _wo_ln(merged, x, w_o, g, b, alpha):
    m, d = x.shape
    tm = _tile(m, 256)
    row = pl.BlockSpec((tm, d), lambda i: (i, 0))
    vec = pl.BlockSpec((1, d), lambda i: (0, 0))
    return pl.pallas_call(
        functools.partial(_wo_ln_body, alpha=alpha), grid=(m // tm,),
        in_specs=[row, row, pl.BlockSpec((d, d), lambda i: (0, 0)), vec, vec],
        out_specs=[row, row],
        out_shape=[jax.ShapeDtypeStruct((m, d), F32), jax.ShapeDtypeStruct((m, d), BF16)],
        compiler_params=_params(1), name="wo_ln")(merged, x, w_o, g, b)


def _ffn_body(xb_ref, xf_ref, wu_ref, wd_ref, g_ref, b_ref, of_ref, ob_ref, acc, *, alpha):
    f = pl.program_id(1)

    @pl.when(f == 0)
    def _():
        acc[...] = jnp.zeros_like(acc)

    h = jnp.maximum(jnp.dot(xb_ref[...], wu_ref[...], preferred_element_type=F32), 0.0)
    acc[...] += jnp.dot((h * h).astype(BF16), wd_ref[...], preferred_element_type=F32)

    @pl.when(f == pl.num_programs(1) - 1)
    def _():
        y = _layernorm(alpha * xf_ref[...] + acc[...], g_ref[...], b_ref[...])
        of_ref[...] = y
        ob_ref[...] = y.astype(BF16)


def _ffn(xb, xf, w_up, w_down, g, b, alpha):
    m, d = xf.shape
    dff = w_up.shape[1]
    tm = _tile(m, 512)
    tf = _tile(dff, 512, LANE)
    row = pl.BlockSpec((tm, d), lambda i, f: (i, 0))
    vec = pl.BlockSpec((1, d), lambda i, f: (0, 0))
    return pl.pallas_call(
        functools.partial(_ffn_body, alpha=alpha), grid=(m // tm, dff // tf),
        in_specs=[row, row, pl.BlockSpec((d, tf), lambda i, f: (0, f)),
                  pl.BlockSpec((tf, d), lambda i, f: (f, 0)), vec, vec],
        out_specs=[row, row],
        out_shape=[jax.ShapeDtypeStruct((m, d), F32), jax.ShapeDtypeStruct((m, d), BF16)],
        scratch_shapes=[pltpu.VMEM((tm, d), F32)],
        compiler_params=_params(2), name="ffn")(xb, xf, w_up, w_down, g, b)


def _prep_weights(w_in, w_branch, w_o, w_up, w_down):
    d = w_in.shape[1]
    sizes = (BR_W, BR_W, BR_W, B_HEADS * HD, HD, HD, IDX_HEADS * IDX_HD, IDX_HD, IDX_HEADS,
             C_HEADS * HD, C_HEADS * HD, C_HEADS * HD, C_HEADS,
             D_HEADS * 2 * D_QK, D_HEADS * 2 * D_QK, D_HEADS * D_V, N_BRANCH * d)
    names = ("a_b", "a_c", "a_h", "b_q", "b_k", "b_v", "b_iq", "b_ik", "b_iw",
             "c_q", "c_k", "c_v", "c_f", "d_q", "d_k", "d_v", "gate")
    starts = np.concatenate([[0], np.cumsum(sizes)])
    col = {n: w_in[:, :, int(starts[i]):int(starts[i + 1])] for i, n in enumerate(names)}
    depth = w_in.shape[0]

    def zeros(n):
        return jnp.zeros((depth, d, n), w_in.dtype)

    w_a = jnp.concatenate([col["a_b"], col["a_c"], col["a_h"]], axis=2)
    w_r = jnp.concatenate([col["b_q"], col["b_iq"], col["b_k"], col["b_ik"], zeros(LANE - IDX_HD),
                           col["d_q"], col["d_k"]], axis=2)
    w_p = jnp.concatenate([col["c_q"], col["c_k"], col["c_v"], col["d_v"], col["b_v"],
                           col["b_iw"], col["c_f"], zeros(LANE - IDX_HEADS - C_HEADS)], axis=2)
    assert w_r.shape[2] == R_BLOCKS * LANE and w_p.shape[2] == P_BLOCKS * LANE
    cast = lambda w: w.astype(BF16)
    return dict(w_a=cast(w_a), w_r=cast(w_r), w_p=cast(w_p), w_g=cast(col["gate"]),
                w_branch=cast(w_branch), w_o=cast(w_o), w_up=cast(w_up), w_down=cast(w_down))


def _cols(a, block, n_blocks=1):
    return a[:, :, block * LANE:(block + n_blocks) * LANE]


def _trunk(x, past_len, conv_state, dsa_rows, fox_rows, diff_rows, wts, conv_w, fox_fbias,
           diff_lambda, diff_gain, ln1_g, ln1_b, ln2_g, ln2_b):
    b, t, d = x.shape
    depth = conv_w.shape[0]
    alpha = (2 * depth) ** 0.25
    m = b * t
    n_keys = past_len + t
    pos = jnp.tile(past_len + jnp.arange(t, dtype=jnp.int32), b)
    tab = _rope_tables(pos)
    if past_len:
        fox_cache = jnp.transpose(fox_rows, (0, 1, 4, 3, 2))
        dsa_cache = jnp.transpose(dsa_rows, (0, 1, 3, 2))
        tq_dsa, tk_dsa = t, -(-n_keys // LANE) * LANE
        new_pad = tk_dsa - past_len

        def new_t(a):
            return jnp.pad(jnp.transpose(a, (0, 2, 1)), ((0, 0), (0, 0), (0, new_pad - t))).astype(BF16)
    else:
        tq_att = _tile(t, 512)
        tk_att = _tile(t, 512, LANE)
        tq_dsa, tk_dsa = _tile(t, 128), _tile(t, 1024, LANE)

    xf = x.reshape(m, d)
    xb = xf.astype(BF16)
    conv_out, dsa_out, fox_out, diff_out = [], [], [], []
    for l in range(depth):
        lam_init = 0.8 - 0.6 * float(np.exp(-0.3 * l))
        u_a = _matmul("proj_conv", xb, wts["w_a"][l])[0].reshape(b, t, 3 * BR_W)
        rf, rb = _matmul("proj_rope", xb, wts["w_r"][l], tab=tab, kinds=R_KINDS, scales=R_SCALES,
                         want_b16=True)
        pf, pb = _matmul("proj_plain", xb, wts["w_p"][l], kinds=P_KINDS, scales=P_SCALES, want_b16=True)
        (gate,) = _matmul("proj_gate", xb, wts["w_g"][l], sigmoid=True, want_f32=False, want_b16=True,
                          tm_target=512)
        rf, rb, pf, pb = (a.reshape(b, t, a.shape[1]) for a in (rf, rb, pf, pb))

        y_a, n_conv = _short_conv(u_a, conv_state[l], conv_w[l])

        small = _cols(pf, P_SMALL)
        bias_row = jnp.zeros((1, LANE), F32).at[0, CF_LANE:CF_LANE + C_HEADS].set(fox_fbias[l])
        if past_len:
            past_lf = jnp.transpose(fox_cache[l, :, 2 * HD], (0, 2, 1))
            past_lf = jnp.pad(past_lf, ((0, 0), (0, 0), (CF_LANE, LANE - CF_LANE - C_HEADS)))
            small = jnp.concatenate([past_lf, small], axis=1)
        lf, cum = _fox_prep(small, bias_row, past_len)
        lf_new = lf[:, past_len:, CF_LANE:CF_LANE + C_HEADS]
        cum = cum[:, :, CF_LANE:CF_LANE + C_HEADS]
        diff_extra = (diff_lambda[l], diff_gain[l].reshape(1, D_V), jnp.full((1, 1), lam_init, F32))
        if past_len:
            ck = jnp.pad(jnp.transpose(cum, (0, 2, 1)), ((0, 0), (0, 0), (0, past_len + LANE - n_keys)))
            y_c = _decode("fox", l, pb, P_CQ // 4, fox_cache, pb, P_CK // 4, pb, P_CV // 4,
                          (ck[:, :, None, :],), past_len=past_len)
            y_d = _decode("diff", l, rb, R_DQ // 4, diff_rows, rb, R_DK // 4, pb, P_DV // 4,
                          diff_extra, past_len=past_len)
            dsa_kv = (dsa_cache, new_t(_cols(rf, R_BK)), new_t(_cols(pf, P_BV)),
                      new_t(_cols(rf, R_BIK)[..., :IDX_HD]))
        else:
            ck = jnp.transpose(cum, (0, 2, 1)).reshape(b, C_HEADS, t // tk_att, 1, tk_att)
            y_c = _flash("fox", pb, P_CQ, pb, P_CK, pb, P_CV, (ck,), tq=tq_att, tk=tk_att)
            y_d = _flash("diff", rb, R_DQ, rb, R_DK, pb, P_DV, diff_extra, tq=tq_att, tk=tk_att)
            dsa_kv = (rb, R_BK, pb, P_BV, rb, R_BIK)
        y_b = _dsa(rb, rf, pf, dsa_kv, layer=l, past_len=past_len, n_keys=n_keys, tq=tq_dsa, tk=tk_dsa)

        ys = [y.reshape(m, BR_W) for y in (y_a, y_b, y_c, y_d)]
        merged = _merge(ys, gate, wts["w_branch"][l])
        xf, xb = _wo_ln(merged, xf, wts["w_o"][l], ln1_g[l].reshape(1, d), ln1_b[l].reshape(1, d), alpha)
        xf, xb = _ffn(xb, xf, wts["w_up"][l], wts["w_down"][l], ln2_g[l].reshape(1, d),
                      ln2_b[l].reshape(1, d), alpha)

        conv_out.append(n_conv)
        dsa_out.append(jnp.concatenate([_cols(rf, R_BK), _cols(pf, P_BV), _cols(rf, R_BIK)[..., :IDX_HD]],
                                       axis=-1))
        fox_out.append(jnp.concatenate([_cols(pf, P_CK, 4).reshape(b, t, C_HEADS, HD),
                                        _cols(pf, P_CV, 4).reshape(b, t, C_HEADS, HD),
                                        lf_new[..., None]], axis=-1))
        diff_out.append(jnp.concatenate([_cols(rf, R_DK, 4).reshape(b, t, D_HEADS, 2 * D_QK),
                                         _cols(pf, P_DV, 4).reshape(b, t, D_HEADS, D_V)], axis=-1))
    return (xf.reshape(b, t, d), jnp.stack(conv_out), jnp.stack(dsa_out), jnp.stack(fox_out),
            jnp.stack(diff_out))


def kernel(x_prompt, x_sample, state_conv, cache_dsa, cache_fox, cache_diff, w_in, conv_w, fox_fbias,
           diff_lambda, diff_gain, w_branch, w_o, ln1_g, ln1_b, w_up, w_down, ln2_g, ln2_b):
    depth = w_in.shape[0]
    b = x_prompt.shape[0]
    wts = _prep_weights(w_in, w_branch, w_o, w_up, w_down)
    shared = (wts, conv_w, fox_fbias, diff_lambda, diff_gain, ln1_g, ln1_b, ln2_g, ln2_b)
    conv0 = jnp.zeros((depth, b, CONV_W - 1, BR_W), x_prompt.dtype)
    y_p, conv_p, dsa_p, fox_p, diff_p = _trunk(x_prompt, 0, conv0, None, None, None, *shared)
    y_s, conv_s, dsa_s, fox_s, diff_s = _trunk(x_sample, cache_dsa.shape[2], state_conv, cache_dsa,
                                               cache_fox, cache_diff, *shared)
    return (y_p, y_s, conv_p, dsa_p, fox_p, diff_p, conv_s, dsa_s, fox_s, diff_s)
```

```python
import functools

import numpy as np
import jax
import jax.numpy as jnp
from jax import lax
from jax.experimental import pallas as pl
from jax.experimental.pallas import tpu as pltpu

CHUNK = 64
N_BRANCH = 4
BR_W = 512
HD = 128
CONV_W = 3
B_HEADS = 4
IDX_HEADS = 4
IDX_HD = 64
TOPK_MAX = 256
IDX_SCALE = (IDX_HEADS * IDX_HD) ** -0.5
C_HEADS = 4
D_HEADS = 4
D_QK = 64
D_V = 128
ROPE_THETA = 500000.0
LN_EPS = 1e-5
RMS_EPS = 1e-5

LANE = 128
STRIP = 16
VMEM_LIMIT = 56 * 1024 * 1024
NEG = -1e30
LOG2E = 1.4426950408889634
INT_MIN = np.int32(-2 ** 31)
F32 = jnp.float32
BF16 = jnp.bfloat16

R_BQ, R_BIQ, R_BK, R_BIK, R_DQ, R_DK, R_BLOCKS = 0, 4, 6, 7, 8, 12, 16
P_CQ, P_CK, P_CV, P_DV, P_BV, P_SMALL, P_BLOCKS = 0, 4, 8, 12, 16, 17, 18
IW_LANE, CF_LANE = 0, IDX_HEADS
R_KINDS = ("r128",) * 4 + ("r64",) * 2 + ("r128", "r64") + ("r64",) * 8
R_SCALES = (HD ** -0.5 * LOG2E,) * 4 + (1.0,) * 4 + (D_QK ** -0.5 * LOG2E,) * 4 + (1.0,) * 4
P_KINDS = (None,) * P_BLOCKS
P_SCALES = (HD ** -0.5 * LOG2E,) * 4 + (1.0,) * 14


def _tile(n, target, align=8):
    if n <= target:
        return n
    for t in range(target - target % align, 0, -align):
        if n % t == 0:
            return t
    return n


def _params(n_grid):
    return pltpu.CompilerParams(dimension_semantics=("arbitrary",) * n_grid,
                                vmem_limit_bytes=VMEM_LIMIT)


def _mm_body(x_ref, w_ref, *rest, kinds, scales, has_tab, sigmoid, want_f32, want_b16):
    tab_ref = rest[0] if has_tab else None
    outs = rest[1:] if has_tab else rest
    of_ref = outs[0] if want_f32 else None
    ob_ref = outs[-1] if want_b16 else None
    acc = jnp.dot(x_ref[...], w_ref[...], preferred_element_type=F32)
    if kinds is None:
        if sigmoid:
            acc = jax.nn.sigmoid(acc)
        if want_f32:
            of_ref[...] = acc
        if want_b16:
            ob_ref[...] = acc.astype(BF16)
        return
    for c, (kind, scale) in enumerate(zip(kinds, scales)):
        sl = slice(c * LANE, (c + 1) * LANE)
        blk = acc[:, sl]
        if kind is not None:
            base = 0 if kind == "r128" else 3 * LANE
            half = (HD // 4) // 2 if kind == "r128" else (IDX_HD // 4) // 2
            cos = tab_ref[:, base:base + LANE]
            s_lo = tab_ref[:, base + LANE:base + 2 * LANE]
            s_hi = tab_ref[:, base + 2 * LANE:base + 3 * LANE]
            blk = (blk * cos + pltpu.roll(blk, LANE - half, 1) * s_lo
                   + pltpu.roll(blk, half, 1) * s_hi)
        if want_f32:
            of_ref[:, sl] = blk
        if want_b16:
            ob_ref[:, sl] = (blk * scale if scale != 1.0 else blk).astype(BF16)


def _matmul(name, x, w, *, tab=None, kinds=None, scales=None, sigmoid=False,
            want_f32=True, want_b16=False, tm_target=256, tn_target=2048):
    m, k = x.shape
    n = w.shape[1]
    tm = _tile(m, tm_target)
    tn = n if kinds is not None else _tile(n, tn_target, LANE)
    grid = (n // tn, m // tm)
    in_specs = [pl.BlockSpec((tm, k), lambda j, i: (i, 0)),
                pl.BlockSpec((k, tn), lambda j, i: (0, j))]
    args = [x, w]
    if tab is not None:
        in_specs.append(pl.BlockSpec((tm, tab.shape[1]), lambda j, i: (i, 0)))
        args.append(tab)
    out_shape, out_specs = [], []
    if want_f32:
        out_shape.append(jax.ShapeDtypeStruct((m, n), F32))
        out_specs.append(pl.BlockSpec((tm, tn), lambda j, i: (i, j)))
    if want_b16:
        out_shape.append(jax.ShapeDtypeStruct((m, n), BF16))
        out_specs.append(pl.BlockSpec((tm, tn), lambda j, i: (i, j)))
    body = functools.partial(_mm_body, kinds=kinds, scales=scales, has_tab=tab is not None,
                             sigmoid=sigmoid, want_f32=want_f32, want_b16=want_b16)
    return pl.pallas_call(body, grid=grid, in_specs=in_specs, out_specs=out_specs,
                          out_shape=out_shape, compiler_params=_params(2), name=name)(*args)


def _rope_tables(pos):
    tabs = []
    for hd in (HD, IDX_HD):
        rot = hd // 4
        half = rot // 2
        freqs = ROPE_THETA ** (-2.0 * jnp.arange(half, dtype=F32) / rot)
        ang = pos.astype(F32)[:, None] * freqs[None, :]
        cos, sin = jnp.cos(ang), jnp.sin(ang)
        m = pos.shape[0]
        one = jnp.ones((m, hd - rot), F32)
        zero_h = jnp.zeros((m, half), F32)
        zero_t = jnp.zeros((m, hd - rot), F32)
        c = jnp.concatenate([cos, cos, one], axis=1)
        s_lo = jnp.concatenate([-sin, zero_h, zero_t], axis=1)
        s_hi = jnp.concatenate([zero_h, sin, zero_t], axis=1)
        rep = LANE // hd
        tabs += [jnp.tile(c, (1, rep)), jnp.tile(s_lo, (1, rep)), jnp.tile(s_hi, (1, rep))]
    return jnp.concatenate(tabs, axis=1)


def _conv_body(u_ref, st_ref, w_ref, y_ref, ns_ref, carry, *, tm):
    @pl.when(pl.program_id(1) == 0)
    def _():
        carry[...] = st_ref[0]

    gate_b = u_ref[0, :, 0:BR_W]
    v = u_ref[0, :, BR_W:2 * BR_W] * u_ref[0, :, 2 * BR_W:3 * BR_W]
    p0 = carry[0:1, :]
    p1 = carry[1:2, :]
    row = lax.broadcasted_iota(jnp.int32, (tm, BR_W), 0)
    v1 = jnp.where(row == 0, p1, pltpu.roll(v, 1, 0))
    v2 = jnp.where(row == 0, p0, jnp.where(row == 1, p1, pltpu.roll(v, 2, 0)))
    w = w_ref[...]
    y = gate_b * (v2 * w[0:1, :] + v1 * w[1:2, :] + v * w[2:3, :])
    y_ref[0] = y.astype(BF16)
    tail = (u_ref[0, tm - 2:tm, BR_W:2 * BR_W] * u_ref[0, tm - 2:tm, 2 * BR_W:3 * BR_W])
    carry[...] = tail
    ns_ref[0] = tail


def _short_conv(u_a, state, conv_w):
    b, t, _ = u_a.shape
    assert t >= CONV_W - 1
    tm = _tile(t, 512)
    body = functools.partial(_conv_body, tm=tm)
    return pl.pallas_call(
        body, grid=(b, t // tm),
        in_specs=[pl.BlockSpec((1, tm, 3 * BR_W), lambda bi, i: (bi, i, 0)),
                  pl.BlockSpec((1, CONV_W - 1, BR_W), lambda bi, i: (bi, 0, 0)),
                  pl.BlockSpec((CONV_W, BR_W), lambda bi, i: (0, 0))],
        out_specs=[pl.BlockSpec((1, tm, BR_W), lambda bi, i: (bi, i, 0)),
                   pl.BlockSpec((1, CONV_W - 1, BR_W), lambda bi, i: (bi, 0, 0))],
        out_shape=[jax.ShapeDtypeStruct((b, t, BR_W), BF16),
                   jax.ShapeDtypeStruct((b, CONV_W - 1, BR_W), F32)],
        scratch_shapes=[pltpu.VMEM((CONV_W - 1, BR_W), F32)],
        compiler_params=_params(2), name="short_conv")(u_a, state, conv_w)


def _fox_prep_body(x_ref, bias_ref, lf_ref, cum_ref, carry, *, tl, past_len):
    i = pl.program_id(1)

    @pl.when(i == 0)
    def _():
        carry[...] = jnp.zeros_like(carry)

    x = x_ref[0]
    z = x + bias_ref[...]
    log_sig = jnp.minimum(z, 0.0) - jnp.log1p(jnp.exp(-jnp.abs(z)))
    row = i * tl + lax.broadcasted_iota(jnp.int32, (tl, 1), 0)
    lf = jnp.where(row >= past_len, log_sig, x)
    r = lax.broadcasted_iota(jnp.int32, (tl, tl), 0)
    c = lax.broadcasted_iota(jnp.int32, (tl, tl), 1)
    tri = jnp.where(c <= r, 1.0, 0.0).astype(F32)
    cs = jnp.dot(tri, lf, precision=lax.Precision.HIGHEST, preferred_element_type=F32) + carry[...]
    lf_ref[0] = lf
    cum_ref[0] = cs
    carry[...] = cs[tl - 1:tl, :]


def _fox_prep(x, bias_row, past_len):
    b, l, _ = x.shape
    tl = _tile(l, 512)
    body = functools.partial(_fox_prep_body, tl=tl, past_len=past_len)
    spec = pl.BlockSpec((1, tl, LANE), lambda bi, i: (bi, i, 0))
    return pl.pallas_call(
        body, grid=(b, l // tl),
        in_specs=[spec, pl.BlockSpec((1, LANE), lambda bi, i: (0, 0))],
        out_specs=[spec, spec],
        out_shape=[jax.ShapeDtypeStruct((b, l, LANE), F32)] * 2,
        scratch_shapes=[pltpu.VMEM((1, LANE), F32)],
        compiler_params=_params(2), name="fox_prep")(x, bias_row)


def _row_limits(qpos, chunk):
    if chunk == 1:
        return qpos + 1
    sh = chunk.bit_length() - 1
    return lax.shift_left(lax.shift_right_arithmetic(qpos, sh) + 1, sh)


def _tile_counts(q0, tq, tk, chunk, n_keys):
    lim_min = (q0 // chunk + 1) * chunk
    lim_max = ((q0 + tq - 1) // chunk + 1) * chunk
    n_full = jnp.minimum(lim_min, n_keys) // tk
    n_need = (jnp.minimum(lim_max, n_keys) + tk - 1) // tk
    return n_full, n_need


def _softmax_scratch(rows, tk):
    return [pltpu.VMEM((rows, LANE), F32)] * 4 + [pltpu.VMEM((2, rows, tk), BF16)]


def _softmax_init(m_sc, l_sc, acc_sc):
    m_sc[...] = jnp.full(m_sc.shape, NEG, F32)
    l_sc[...] = jnp.zeros(l_sc.shape, F32)
    acc_sc[...] = jnp.zeros(acc_sc.shape, F32)


def _softmax_tile(s, pv, state, fix=None, base=0, slot=0):
    m_sc, l_sc, a_sc, acc_sc, p_sc = state
    rows, tk = s.shape
    for r0 in range(0, rows, STRIP):
        rs = slice(base + r0, base + r0 + STRIP)
        x = s[r0:r0 + STRIP, :]
        if fix is not None:
            x = fix(x, r0)
        blocks = [x[:, c * LANE:(c + 1) * LANE] for c in range(tk // LANE)]
        mx = blocks[0]
        for blk in blocks[1:]:
            mx = jnp.maximum(mx, blk)
        m_old = m_sc[rs, :]
        m_new = jnp.maximum(m_old, jnp.max(mx, axis=1, keepdims=True))
        alpha = jnp.exp2(m_old - m_new)
        tot = None
        for c, blk in enumerate(blocks):
            p = jnp.exp2(blk - m_new)
            tot = p if tot is None else tot + p
            p_sc[slot, rs, c * LANE:(c + 1) * LANE] = p.astype(BF16)
        l_sc[rs, :] = alpha * l_sc[rs, :] + tot
        m_sc[rs, :] = m_new
        a_sc[rs, :] = alpha
    rs = slice(base, base + rows)
    acc_sc[rs, :] = a_sc[rs, :] * acc_sc[rs, :] + pv(p_sc[slot, rs, 0:tk])


def _pv(v):
    return lambda p: jnp.dot(p, v, preferred_element_type=F32)


def _softmax_out(state):
    _, l_sc, _, acc_sc, _ = state
    return acc_sc[...] / jnp.sum(l_sc[...], axis=1, keepdims=True)


def _qk(q, k):
    return lax.dot_general(q, k, (((1,), (1,)), ((), ())), preferred_element_type=F32)


def _split_subheads(q):
    qf = q.astype(F32)
    lane = lax.broadcasted_iota(jnp.int32, qf.shape, 1)
    return jnp.concatenate([jnp.where(lane < D_QK, qf, 0.0), jnp.where(lane >= D_QK, qf, 0.0)],
                           axis=0).astype(BF16)


def _diff_out(o, tq, lam_ref, gain_ref, li_ref):
    lam = lam_ref[...]
    li = li_ref[...]
    lmb = (jnp.exp(jnp.sum(lam[0:1, :] * lam[1:2, :], axis=1, keepdims=True))
           - jnp.exp(jnp.sum(lam[2:3, :] * lam[3:4, :], axis=1, keepdims=True)) + li)
    o = o[0:tq, :] - lmb * o[tq:2 * tq, :]
    o = o * lax.rsqrt(jnp.mean(o * o, axis=1, keepdims=True) + RMS_EPS)
    return o * gain_ref[...] * (1.0 - li)


def _flash_body(*refs, tq, tk, chunk, mode):
    if mode == "fox":
        q_ref, k_ref, v_ref, ck_ref = refs[:4]
        o_ref, state = refs[4], refs[5:]
    else:
        q_ref, k_ref, v_ref, lam_ref, gain_ref, li_ref = refs[:6]
        o_ref, state = refs[6], refs[7:]
    n_keys = k_ref.shape[1]
    q0 = pl.program_id(2) * tq
    n_full, n_need = _tile_counts(q0, tq, tk, chunk, n_keys)
    qlim = _row_limits(q0 + lax.broadcasted_iota(jnp.int32, (tq, 1), 0), chunk)
    if mode == "fox":
        q = q_ref[0]
    else:
        q = _split_subheads(q_ref[0])
        qlim = jnp.concatenate([qlim, qlim], axis=0)
    _softmax_init(state[0], state[1], state[3])

    def step(j, masked, slot):
        ks = pl.multiple_of(j * tk, tk)
        s = _qk(q, k_ref[0, pl.ds(ks, tk), :])
        ck = ck_ref[0, 0, j] * LOG2E if mode == "fox" else None
        kpos = ks + lax.broadcasted_iota(jnp.int32, (1, tk), 1)

        def fix(x, r0):
            if ck is not None:
                x = x - ck
            if masked:
                x = jnp.where(kpos < qlim[r0:r0 + STRIP, :], x, NEG)
            return x

        _softmax_tile(s, _pv(v_ref[0, pl.ds(ks, tk), :]), state,
                      fix if (masked or ck is not None) else None, slot=slot)

    def full_pair(jj, c):
        step(2 * jj, False, 0)
        step(2 * jj + 1, False, 1)
        return c

    def masked_step(j, c):
        step(j, True, 0)
        return c

    lax.fori_loop(0, n_full // 2, full_pair, 0)

    @pl.when(n_full % 2 == 1)
    def _():
        step(n_full - 1, False, 0)

    lax.fori_loop(n_full, n_need, masked_step, 0)

    o = _softmax_out(state)
    if mode == "diff":
        o = _diff_out(o, tq, lam_ref, gain_ref, li_ref)
    o_ref[0] = o.astype(BF16)


def _flash(mode, q, q_off, k, k_off, v, v_off, extra, *, tq, tk):
    b, t, _ = q.shape
    heads = C_HEADS if mode == "fox" else D_HEADS
    stack = 1 if mode == "fox" else 2
    chunk = 1 if mode == "fox" else CHUNK
    in_specs = [pl.BlockSpec((1, tq, LANE), lambda bi, h, i: (bi, i, q_off + h)),
                pl.BlockSpec((1, t, LANE), lambda bi, h, i: (bi, 0, k_off + h)),
                pl.BlockSpec((1, t, LANE), lambda bi, h, i: (bi, 0, v_off + h))]
    if mode == "fox":
        in_specs.append(pl.BlockSpec((1, 1, t // tk, 1, tk), lambda bi, h, i: (bi, h, 0, 0, 0)))
    else:
        in_specs += [pl.BlockSpec(e.shape, lambda bi, h, i: (0, 0)) for e in extra]
    body = functools.partial(_flash_body, tq=tq, tk=tk, chunk=chunk, mode=mode)
    return pl.pallas_call(
        body, grid=(b, heads, t // tq), in_specs=in_specs,
        out_specs=pl.BlockSpec((1, tq, LANE), lambda bi, h, i: (bi, i, h)),
        out_shape=jax.ShapeDtypeStruct((b, t, heads * LANE), BF16),
        scratch_shapes=_softmax_scratch(stack * tq, tk),
        compiler_params=_params(3), name=mode + "_prefill")(q, k, v, *extra)


def _decode_body(*refs, t, past_len, mode):
    if mode == "fox":
        q_ref, cache_ref, kn_ref, vn_ref, ck_ref = refs[:5]
        o_ref, state = refs[5], refs[6:]
    else:
        q_ref, cache_ref, kn_ref, vn_ref, lam_ref, gain_ref, li_ref = refs[:7]
        o_ref, state = refs[7], refs[8:]
    heads = C_HEADS
    stack = 1 if mode == "fox" else 2
    rows = stack * t
    tt = lax.broadcasted_iota(jnp.int32, (t, 1), 0)
    qlim = jnp.minimum(_row_limits(past_len + tt, 1 if mode == "fox" else CHUNK), past_len + t)
    qlim = jnp.concatenate([qlim] * stack, axis=0)
    new_pos = past_len + lax.broadcasted_iota(jnp.int32, (1, LANE), 1)
    zpad = jnp.zeros((LANE - t, LANE), BF16)
    _softmax_init(state[0], state[1], state[3])

    for h in range(heads):
        hs = slice(h * LANE, (h + 1) * LANE)
        base = h * rows
        qh = q_ref[0, :, hs]
        if mode == "fox":
            kt = cache_ref[0, 0, 0:HD, h, :].astype(BF16)
            vt = cache_ref[0, 0, HD:2 * HD, h, :].astype(BF16)
            ck_past = ck_ref[0, h, :, 0:past_len] * LOG2E
            ck_new = ck_ref[0, h, :, past_len:past_len + LANE] * LOG2E
            s = jnp.dot(qh, kt, preferred_element_type=F32)
            _softmax_tile(s, lambda p, vt=vt: _qk(p, vt), state,
                          lambda x, r0, ck_past=ck_past: x - ck_past, base)
        else:
            qh = _split_subheads(qh)
            k = cache_ref[0, 0, :, h, 0:2 * D_QK].astype(BF16)
            v = cache_ref[0, 0, :, h, 2 * D_QK:].astype(BF16)
            ck_new = None
            _softmax_tile(_qk(qh, k), _pv(v), state, None, base)

        def fix(x, r0, ck_new=ck_new):
            if ck_new is not None:
                x = x - ck_new
            return jnp.where(new_pos < qlim[r0:r0 + STRIP, :], x, NEG)

        kn = jnp.concatenate([kn_ref[0, :, hs], zpad], axis=0)
        vn = jnp.concatenate([vn_ref[0, :, hs], zpad], axis=0)
        _softmax_tile(_qk(qh, kn), _pv(vn), state, fix, base)

    o = _softmax_out(state)
    for h in range(heads):
        oh = o[h * rows:(h + 1) * rows, :]
        if mode == "diff":
            oh = _diff_out(oh, t, lam_ref, gain_ref, li_ref)
        o_ref[0, :, h * LANE:(h + 1) * LANE] = oh.astype(BF16)


def _decode(mode, layer, q, q_blk, cache, k_new, k_blk, v_new, v_blk, extra, *, past_len):
    b, t, _ = q.shape
    heads = C_HEADS
    assert t <= LANE and t % STRIP == 0 and past_len % LANE == 0
    stack = 1 if mode == "fox" else 2
    wide = lambda blk: pl.BlockSpec((1, t, heads * LANE), lambda bi: (bi, 0, blk))
    in_specs = [wide(q_blk), pl.BlockSpec((1, 1) + cache.shape[2:], lambda bi: (layer, bi, 0, 0, 0)),
                wide(k_blk), wide(v_blk)]
    if mode == "fox":
        in_specs.append(pl.BlockSpec((1,) + extra[0].shape[1:], lambda bi: (bi, 0, 0, 0)))
    else:
        in_specs += [pl.BlockSpec(e.shape, lambda bi: (0, 0)) for e in extra]
    body = functools.partial(_decode_body, t=t, past_len=past_len, mode=mode)
    return pl.pallas_call(
        body, grid=(b,), in_specs=in_specs, out_specs=wide(0),
        out_shape=jax.ShapeDtypeStruct((b, t, heads * LANE), BF16),
        scratch_shapes=_softmax_scratch(stack * heads * t, past_len),
        compiler_params=_params(1), name=mode + "_decode")(q, cache, k_new, v_new, *extra)


def _dsa_body(*refs, tq, tk, past_len, n_keys, k_sel, cached):
    if cached:
        (q_ref, iq_ref, iw_ref, cache_ref, knt_ref, vnt_ref, iknt_ref, o_ref,
         keys, keys_hi, bias_sc, kt_all, vt_all, ikt_all) = refs[:14]
        state = refs[14:]
        for dst, new, lo, hi in ((kt_all, knt_ref, 0, HD), (vt_all, vnt_ref, HD, 2 * HD),
                                 (ikt_all, iknt_ref, 2 * HD, 2 * HD + IDX_HD)):
            dst[:, 0:past_len] = cache_ref[0, 0, lo:hi, :].astype(BF16)
            dst[:, past_len:] = new[0]
        index_logits = lambda iq, ks: jnp.dot(iq[:, 0:IDX_HD], ikt_all[...], preferred_element_type=F32)
        key_logits = lambda q, ks: jnp.dot(q, kt_all[...], preferred_element_type=F32)
        values = lambda ks: (lambda p: _qk(p, vt_all[...]))
    else:
        q_ref, iq_ref, iw_ref, k_ref, v_ref, ik_ref, o_ref, keys, keys_hi, bias_sc = refs[:10]
        state = refs[10:]
        index_logits = lambda iq, ks: _qk(iq, ik_ref[0, pl.ds(ks, tk), :])
        key_logits = lambda q, ks: _qk(q, k_ref[0, pl.ds(ks, tk), :])
        values = lambda ks: _pv(v_ref[0, pl.ds(ks, tk), :])

    q0 = past_len + pl.program_id(1) * tq
    _, n_need = _tile_counts(q0, tq, tk, CHUNK, n_keys)
    qlim = jnp.minimum(_row_limits(q0 + lax.broadcasted_iota(jnp.int32, (tq, 1), 0), CHUNK), n_keys)

    iqf = iq_ref[0]
    parts = []
    for h in range(IDX_HEADS):
        blk = iqf[:, (h // 2) * LANE:(h // 2 + 1) * LANE]
        parts.append(pltpu.roll(blk, IDX_HD, 1) if h % 2 else blk)
    iq = jnp.concatenate([p[r0:r0 + STRIP, :] for r0 in range(0, tq, STRIP) for p in parts],
                         axis=0).astype(BF16)
    iw = iw_ref[0]
    fold_scale = np.frexp(IDX_SCALE)[0] == 0.5
    w_b = [jnp.broadcast_to(iw[:, IW_LANE + h:IW_LANE + h + 1] * (IDX_SCALE if fold_scale else 1.0),
                            (tq, LANE)) for h in range(IDX_HEADS)]
    qlim_b = jnp.broadcast_to(qlim, (tq, LANE))
    lane = lax.broadcasted_iota(jnp.int32, (1, LANE), 1)

    def score_tile(j, c):
        ks = pl.multiple_of(j * tk, tk)
        rel = index_logits(iq, ks)
        for r0 in range(0, tq, STRIP):
            rs = slice(r0, r0 + STRIP)
            for c0 in range(0, tk, LANE):
                cs = slice(c0, c0 + LANE)
                sc = None
                for h in range(IDX_HEADS):
                    first = (r0 * IDX_HEADS) + h * STRIP
                    term = jnp.maximum(rel[first:first + STRIP, cs], 0.0) * w_b[h][rs, :]
                    sc = term if sc is None else sc + term
                if not fold_scale:
                    sc = sc * IDX_SCALE
                sc = jnp.where(sc == 0.0, 0.0, sc)
                bits = lax.bitcast_convert_type(sc, jnp.int32)
                key = jnp.where(bits < 0, bits ^ jnp.int32(0x7FFFFFFF), bits)
                key = jnp.where(ks + c0 + lane < qlim_b[rs, :], key, INT_MIN)
                keys[j, rs, cs] = key
                keys_hi[j, rs, cs] = lax.shift_right_arithmetic(key, 16).astype(jnp.int16)
        return c

    lax.fori_loop(0, n_need, score_tile, 0)

    def count(pred):
        def add(j, acc):
            hit = jnp.where(pred(keys[j]), 1.0, 0.0)
            for c in range(tk // LANE):
                acc = acc + hit[:, c * LANE:(c + 1) * LANE]
            return acc

        def pair(jj, acc):
            return add(2 * jj + 1, add(2 * jj, acc))

        acc = lax.fori_loop(0, n_need // 2, pair, jnp.zeros((tq, LANE), F32))
        acc = lax.cond(n_need % 2 == 1, lambda a: add(n_need - 1, a), lambda a: a, acc)
        return jnp.sum(acc, axis=1, keepdims=True)

    def count_ge(cand):
        return count(lambda t: t >= cand)

    def count_hi_ge(cand):
        cand = jnp.broadcast_to(cand, (tq, LANE)).astype(jnp.int16)
        one = jnp.ones((tq, LANE), BF16)

        def add(j, acc):
            t = keys_hi[j]
            for c in range(tk // LANE):
                acc = acc + jnp.where(t[:, c * LANE:(c + 1) * LANE] >= cand, one, jnp.zeros_like(one))
            return acc

        acc = lax.fori_loop(0, n_need, add, jnp.zeros((tq, LANE), BF16))
        return jnp.sum(acc.astype(F32), axis=1, keepdims=True)

    def any_row(flag):
        return (jnp.max(jnp.where(flag, 1.0, 0.0)) > 0.0).astype(jnp.int32)

    want = jnp.float32(k_sel)
    crowded = qlim.astype(F32) > want

    zero = jnp.zeros((tq, 1), jnp.int32)
    n_nonneg = count_hi_ge(zero)
    n_pos = count_ge(zero + 1)
    at_zero = crowded & (n_pos < want) & (n_nonneg >= want)
    searching = crowded & jnp.logical_not(at_zero)
    half = 2 ** 15

    def upper_bit(b, st):
        ans, n_ans = st
        cand = ans + lax.shift_left(jnp.int32(1), 15 - b)
        n = count_hi_ge(cand)
        take = (n >= want) & searching
        return jnp.where(take, cand, ans), jnp.where(take, n, n_ans)

    top, n_top = lax.fori_loop(0, 16, upper_bit, (jnp.full((tq, 1), -half, jnp.int32),
                                                  jnp.full((tq, 1), 2.0 * k_sel + 1.0, F32)))
    go = any_row(searching & (n_top != want))
    n_above = jnp.where(top < half - 1, count_hi_ge(jnp.minimum(top + 1, half - 1)), 0.0)

    @pl.when(go > 0)
    def _():
        top_b = jnp.broadcast_to(top, (tq, LANE)).astype(jnp.int16)
        floor = jnp.full((tq, LANE), -half, jnp.int16)

        def keep_bucket(j, c):
            low = ((keys[j] & jnp.int32(0xFFFF)) - half).astype(jnp.int16)
            upper = keys_hi[j]
            for c0 in range(0, tk, LANE):
                cs = slice(c0, c0 + LANE)
                keys_hi[j, :, cs] = jnp.where(upper[:, cs] == top_b, low[:, cs], floor)
            return c

        lax.fori_loop(0, n_need, keep_bucket, 0)

    def lower_bits(st):
        b, _, ans, n_ans = st
        for d in range(2):
            cand = ans + lax.shift_left(jnp.int32(1), 15 - b - d)
            n = n_above + count_hi_ge(cand - half)
            take = (n >= want) & searching
            ans = jnp.where(take, cand, ans)
            n_ans = jnp.where(take, n, n_ans)
        return b + 2, any_row(searching & (n_ans != want)), ans, n_ans

    _, _, low, n_low = lax.while_loop(lambda st: (st[0] < 16) & (st[1] > 0), lower_bits,
                                      (jnp.int32(0), go, zero, n_top))
    kstar = jnp.where(at_zero, zero, lax.shift_left(top, 16) + low)
    n_ge = jnp.where(at_zero, n_nonneg, n_low)
    tie = crowded & (n_ge > want)

    @pl.when(any_row(tie) > 0)
    def _():
        n_gt = lax.cond(any_row(tie & jnp.logical_not(at_zero)) > 0, lambda: count_ge(kstar + 1),
                        lambda: jnp.zeros((tq, 1), F32))
        need = want - jnp.where(at_zero, n_pos, n_gt)
        pw = next(w for w in (512, 256, LANE) if tk % w == 0)
        r = lax.broadcasted_iota(jnp.int32, (pw, pw), 0)
        c = lax.broadcasted_iota(jnp.int32, (pw, pw), 1)
        prefix = jnp.where(r <= c, 1.0, 0.0).astype(BF16)

        def drop(j, seen):
            t = keys[j]
            eq = tie & (t == kstar)
            kept = []
            for c0 in range(0, tk, pw):
                eq_c = eq[:, c0:c0 + pw]
                rank = seen + jnp.dot(jnp.where(eq_c, 1.0, 0.0).astype(BF16), prefix,
                                      preferred_element_type=F32)
                kept.append(jnp.where(eq_c & (rank > need), INT_MIN, t[:, c0:c0 + pw]))
                seen = rank[:, pw - 1:pw]
            keys[j] = kept[0] if len(kept) == 1 else jnp.concatenate(kept, axis=1)
            return seen

        lax.fori_loop(0, n_need, drop, jnp.zeros((tq, 1), F32))

    thr = jnp.maximum(kstar, INT_MIN + 1)
    qf = q_ref[0]
    q = jnp.concatenate([qf[:, h * LANE:(h + 1) * LANE] for h in range(B_HEADS)], axis=0)
    _softmax_init(state[0], state[1], state[3])

    def attend(j, slot):
        ks = pl.multiple_of(j * tk, tk)
        bias_sc[slot] = jnp.where(keys[j] >= thr, 0.0, NEG)
        _softmax_tile(key_logits(q, ks), values(ks), state,
                      lambda x, r0: x + bias_sc[slot, r0 % tq:r0 % tq + STRIP, :], slot=slot)

    def attend_pair(jj, c):
        attend(2 * jj, 0)
        attend(2 * jj + 1, 1)
        return c

    lax.fori_loop(0, n_need // 2, attend_pair, 0)

    @pl.when(n_need % 2 == 1)
    def _():
        attend(n_need - 1, 0)

    o = _softmax_out(state)
    for h in range(B_HEADS):
        o_ref[0, :, h * LANE:(h + 1) * LANE] = o[h * tq:(h + 1) * tq, :].astype(BF16)


def _dsa(q, iq, iw, kv, *, layer, past_len, n_keys, tq, tk):
    b, t, _ = q.shape
    cached = past_len > 0
    lpad = -(-n_keys // tk) * tk
    assert not cached or (lpad == tk and past_len % LANE == 0)
    k_sel = min(TOPK_MAX, n_keys // 4)
    body = functools.partial(_dsa_body, tq=tq, tk=tk, past_len=past_len, n_keys=n_keys, k_sel=k_sel,
                             cached=cached)
    rows = B_HEADS * tq
    in_specs = [pl.BlockSpec((1, tq, B_HEADS * LANE), lambda bi, i: (bi, i, R_BQ // B_HEADS)),
                pl.BlockSpec((1, tq, 2 * LANE), lambda bi, i: (bi, i, R_BIQ // 2)),
                pl.BlockSpec((1, tq, LANE), lambda bi, i: (bi, i, P_SMALL))]
    assert lpad // LANE <= 256
    scratch = [pltpu.VMEM((lpad // tk, tq, tk), jnp.int32), pltpu.VMEM((lpad // tk, tq, tk), jnp.int16),
               pltpu.VMEM((2, tq, tk), F32)]
    if cached:
        cache = kv[0]
        in_specs.append(pl.BlockSpec((1, 1) + cache.shape[2:], lambda bi, i: (layer, bi, 0, 0)))
        in_specs += [pl.BlockSpec((1,) + a.shape[1:], lambda bi, i: (bi, 0, 0)) for a in kv[1:]]
        args = kv
        scratch += [pltpu.VMEM((a.shape[1], lpad), BF16) for a in kv[1:]]
    else:
        k, k_blk, v, v_blk, ik, ik_blk = kv
        in_specs += [pl.BlockSpec((1, n_keys, LANE), lambda bi, i: (bi, 0, k_blk)),
                     pl.BlockSpec((1, n_keys, LANE), lambda bi, i: (bi, 0, v_blk)),
                     pl.BlockSpec((1, n_keys, LANE), lambda bi, i: (bi, 0, ik_blk))]
        args = (k, v, ik)
    return pl.pallas_call(
        body, grid=(b, t // tq), in_specs=in_specs,
        out_specs=pl.BlockSpec((1, tq, B_HEADS * LANE), lambda bi, i: (bi, i, 0)),
        out_shape=jax.ShapeDtypeStruct((b, t, B_HEADS * LANE), BF16),
        scratch_shapes=scratch + _softmax_scratch(rows, tk),
        compiler_params=_params(2), name="dsa_decode" if cached else "dsa_prefill")(q, iq, iw, *args)


def _merge_body(ya_ref, yb_ref, yc_ref, yd_ref, g_ref, wb_ref, o_ref):
    d = o_ref.shape[1]
    acc = 0.0
    for m, y_ref in enumerate((ya_ref, yb_ref, yc_ref, yd_ref)):
        acc = acc + g_ref[:, m * d:(m + 1) * d].astype(F32) * jnp.dot(
            y_ref[...], wb_ref[m], preferred_element_type=F32)
    o_ref[...] = acc.astype(BF16)


def _merge(ys, gate, w_branch):
    m, d = gate.shape[0], w_branch.shape[2]
    tm = _tile(m, 256)
    y_spec = pl.BlockSpec((tm, BR_W), lambda i: (i, 0))
    return pl.pallas_call(
        _merge_body, grid=(m // tm,),
        in_specs=[y_spec] * N_BRANCH + [pl.BlockSpec((tm, N_BRANCH * d), lambda i: (i, 0)),
                                        pl.BlockSpec(w_branch.shape, lambda i: (0, 0, 0))],
        out_specs=pl.BlockSpec((tm, d), lambda i: (i, 0)),
        out_shape=jax.ShapeDtypeStruct((m, d), BF16),
        compiler_params=_params(1), name="merge")(*ys, gate, w_branch)


def _layernorm(z, g, b):
    mu = jnp.mean(z, axis=-1, keepdims=True)
    zc = z - mu
    var = jnp.mean(zc * zc, axis=-1, keepdims=True)
    return zc * lax.rsqrt(var + LN_EPS) * g + b


def _wo_ln_body(m_ref, x_ref, w_ref, g_ref, b_ref, of_ref, ob_ref, *, alpha):
    h = jnp.dot(m_ref[...], w_ref[...], preferred_element_type=F32)
    y = _layernorm(alpha * x_ref[...] + h, g_ref[...], b_ref[...])
    of_ref[...] = y
    ob_ref[...] = y.astype(BF16)


def _wo_ln(merged, x, w_o, g, b, alpha):
    m, d = x.shape
    tm = _tile(m, 256)
    row = pl.BlockSpec((tm, d), lambda i: (i, 0))
    vec = pl.BlockSpec((1, d), lambda i: (0, 0))
    return pl.pallas_call(
        functools.partial(_wo_ln_body, alpha=alpha), grid=(m // tm,),
        in_specs=[row, row, pl.BlockSpec((d, d), lambda i: (0, 0)), vec, vec],
        out_specs=[row, row],
        out_shape=[jax.ShapeDtypeStruct((m, d), F32), jax.ShapeDtypeStruct((m, d), BF16)],
        compiler_params=_params(1), name="wo_ln")(merged, x, w_o, g, b)


def _ffn_body(xb_ref, xf_ref, wu_ref, wd_ref, g_ref, b_ref, of_ref, ob_ref, acc, *, alpha):
    f = pl.program_id(1)

    @pl.when(f == 0)
    def _():
        acc[...] = jnp.zeros_like(acc)

    h = jnp.maximum(jnp.dot(xb_ref[...], wu_ref[...], preferred_element_type=F32), 0.0)
    acc[...] += jnp.dot((h * h).astype(BF16), wd_ref[...], preferred_element_type=F32)

    @pl.when(f == pl.num_programs(1) - 1)
    def _():
        y = _layernorm(alpha * xf_ref[...] + acc[...], g_ref[...], b_ref[...])
        of_ref[...] = y
        ob_ref[...] = y.astype(BF16)


def _ffn(xb, xf, w_up, w_down, g, b, alpha):
    m, d = xf.shape
    dff = w_up.shape[1]
    tm = _tile(m, 512)
    tf = _tile(dff, 512, LANE)
    row = pl.BlockSpec((tm, d), lambda i, f: (i, 0))
    vec = pl.BlockSpec((1, d), lambda i, f: (0, 0))
    return pl.pallas_call(
        functools.partial(_ffn_body, alpha=alpha), grid=(m // tm, dff // tf),
        in_specs=[row, row, pl.BlockSpec((d, tf), lambda i, f: (0, f)),
                  pl.BlockSpec((tf, d), lambda i, f: (f, 0)), vec, vec],
        out_specs=[row, row],
        out_shape=[jax.ShapeDtypeStruct((m, d), F32), jax.ShapeDtypeStruct((m, d), BF16)],
        scratch_shapes=[pltpu.VMEM((tm, d), F32)],
        compiler_params=_params(2), name="ffn")(xb, xf, w_up, w_down, g, b)


def _prep_weights(w_in, w_branch, w_o, w_up, w_down):
    d = w_in.shape[1]
    sizes = (BR_W, BR_W, BR_W, B_HEADS * HD, HD, HD, IDX_HEADS * IDX_HD, IDX_HD, IDX_HEADS,
             C_HEADS * HD, C_HEADS * HD, C_HEADS * HD, C_HEADS,
             D_HEADS * 2 * D_QK, D_HEADS * 2 * D_QK, D_HEADS * D_V, N_BRANCH * d)
    names = ("a_b", "a_c", "a_h", "b_q", "b_k", "b_v", "b_iq", "b_ik", "b_iw",
             "c_q", "c_k", "c_v", "c_f", "d_q", "d_k", "d_v", "gate")
    starts = np.concatenate([[0], np.cumsum(sizes)])
    col = {n: w_in[:, :, int(starts[i]):int(starts[i + 1])] for i, n in enumerate(names)}
    depth = w_in.shape[0]

    def zeros(n):
        return jnp.zeros((depth, d, n), w_in.dtype)

    w_a = jnp.concatenate([col["a_b"], col["a_c"], col["a_h"]], axis=2)
    w_r = jnp.concatenate([col["b_q"], col["b_iq"], col["b_k"], col["b_ik"], zeros(LANE - IDX_HD),
                           col["d_q"], col["d_k"]], axis=2)
    w_p = jnp.concatenate([col["c_q"], col["c_k"], col["c_v"], col["d_v"], col["b_v"],
                           col["b_iw"], col["c_f"], zeros(LANE - IDX_HEADS - C_HEADS)], axis=2)
    assert w_r.shape[2] == R_BLOCKS * LANE and w_p.shape[2] == P_BLOCKS * LANE
    cast = lambda w: w.astype(BF16)
    return dict(w_a=cast(w_a), w_r=cast(w_r), w_p=cast(w_p), w_g=cast(col["gate"]),
                w_branch=cast(w_branch), w_o=cast(w_o), w_up=cast(w_up), w_down=cast(w_down))


def _cols(a, block, n_blocks=1):
    return a[:, :, block * LANE:(block + n_blocks) * LANE]


def _trunk(x, past_len, conv_state, dsa_rows, fox_rows, diff_rows, wts, conv_w, fox_fbias,
           diff_lambda, diff_gain, ln1_g, ln1_b, ln2_g, ln2_b):
    b, t, d = x.shape
    depth = conv_w.shape[0]
    alpha = (2 * depth) ** 0.25
    m = b * t
    n_keys = past_len + t
    pos = jnp.tile(past_len + jnp.arange(t, dtype=jnp.int32), b)
    tab = _rope_tables(pos)
    if past_len:
        fox_cache = jnp.transpose(fox_rows, (0, 1, 4, 3, 2))
        dsa_cache = jnp.transpose(dsa_rows, (0, 1, 3, 2))
        tq_dsa, tk_dsa = t, -(-n_keys // LANE) * LANE
        new_pad = tk_dsa - past_len

        def new_t(a):
            return jnp.pad(jnp.transpose(a, (0, 2, 1)), ((0, 0), (0, 0), (0, new_pad - t))).astype(BF16)
    else:
        tq_att = _tile(t, 512)
        tk_att = _tile(t, 512, LANE)
        tq_dsa, tk_dsa = _tile(t, 128), _tile(t, 1024, LANE)

    xf = x.reshape(m, d)
    xb = xf.astype(BF16)
    conv_out, dsa_out, fox_out, diff_out = [], [], [], []
    for l in range(depth):
        lam_init = 0.8 - 0.6 * float(np.exp(-0.3 * l))
        u_a = _matmul("proj_conv", xb, wts["w_a"][l])[0].reshape(b, t, 3 * BR_W)
        rf, rb = _matmul("proj_rope", xb, wts["w_r"][l], tab=tab, kinds=R_KINDS, scales=R_SCALES,
                         want_b16=True)
        pf, pb = _matmul("proj_plain", xb, wts["w_p"][l], kinds=P_KINDS, scales=P_SCALES, want_b16=True)
        (gate,) = _matmul("proj_gate", xb, wts["w_g"][l], sigmoid=True, want_f32=False, want_b16=True,
                          tm_target=512)
        rf, rb, pf, pb = (a.reshape(b, t, a.shape[1]) for a in (rf, rb, pf, pb))

        y_a, n_conv = _short_conv(u_a, conv_state[l], conv_w[l])

        small = _cols(pf, P_SMALL)
        bias_row = jnp.zeros((1, LANE), F32).at[0, CF_LANE:CF_LANE + C_HEADS].set(fox_fbias[l])
        if past_len:
            past_lf = jnp.transpose(fox_cache[l, :, 2 * HD], (0, 2, 1))
            past_lf = jnp.pad(past_lf, ((0, 0), (0, 0), (CF_LANE, LANE - CF_LANE - C_HEADS)))
            small = jnp.concatenate([past_lf, small], axis=1)
        lf, cum = _fox_prep(small, bias_row, past_len)
        lf_new = lf[:, past_len:, CF_LANE:CF_LANE + C_HEADS]
        cum = cum[:, :, CF_LANE:CF_LANE + C_HEADS]
        diff_extra = (diff_lambda[l], diff_gain[l].reshape(1, D_V), jnp.full((1, 1), lam_init, F32))
        if past_len:
            ck = jnp.pad(jnp.transpose(cum, (0, 2, 1)), ((0, 0), (0, 0), (0, past_len + LANE - n_keys)))
            y_c = _decode("fox", l, pb, P_CQ // 4, fox_cache, pb, P_CK // 4, pb, P_CV // 4,
                          (ck[:, :, None, :],), past_len=past_len)
            y_d = _decode("diff", l, rb, R_DQ // 4, diff_rows, rb, R_DK // 4, pb, P_DV // 4,
                          diff_extra, past_len=past_len)
            dsa_kv = (dsa_cache, new_t(_cols(rf, R_BK)), new_t(_cols(pf, P_BV)),
                      new_t(_cols(rf, R_BIK)[..., :IDX_HD]))
        else:
            ck = jnp.transpose(cum, (0, 2, 1)).reshape(b, C_HEADS, t // tk_att, 1, tk_att)
            y_c = _flash("fox", pb, P_CQ, pb, P_CK, pb, P_CV, (ck,), tq=tq_att, tk=tk_att)
            y_d = _flash("diff", rb, R_DQ, rb, R_DK, pb, P_DV, diff_extra, tq=tq_att, tk=tk_att)
            dsa_kv = (rb, R_BK, pb, P_BV, rb, R_BIK)
        y_b = _dsa(rb, rf, pf, dsa_kv, layer=l, past_len=past_len, n_keys=n_keys, tq=tq_dsa, tk=tk_dsa)

        ys = [y.reshape(m, BR_W) for y in (y_a, y_b, y_c, y_d)]
        merged = _merge(ys, gate, wts["w_branch"][l])
        xf, xb = _wo_ln(merged, xf, wts["w_o"][l], ln1_g[l].reshape(1, d), ln1_b[l].reshape(1, d), alpha)
        xf, xb = _ffn(xb, xf, wts["w_up"][l], wts["w_down"][l], ln2_g[l].reshape(1, d),
                      ln2_b[l].reshape(1, d), alpha)

        conv_out.append(n_conv)
        dsa_out.append(jnp.concatenate([_cols(rf, R_BK), _cols(pf, P_BV), _cols(rf, R_BIK)[..., :IDX_HD]],
                                       axis=-1))
        fox_out.append(jnp.concatenate([_cols(pf, P_CK, 4).reshape(b, t, C_HEADS, HD),
                                        _cols(pf, P_CV, 4).reshape(b, t, C_HEADS, HD),
                                        lf_new[..., None]], axis=-1))
        diff_out.append(jnp.concatenate([_cols(rf, R_DK, 4).reshape(b, t, D_HEADS, 2 * D_QK),
                                         _cols(pf, P_DV, 4).reshape(b, t, D_HEADS, D_V)], axis=-1))
    return (xf.reshape(b, t, d), jnp.stack(conv_out), jnp.stack(dsa_out), jnp.stack(fox_out),
            jnp.stack(diff_out))


def kernel(x_prompt, x_sample, state_conv, cache_dsa, cache_fox, cache_diff, w_in, conv_w, fox_fbias,
           diff_lambda, diff_gain, w_branch, w_o, ln1_g, ln1_b, w_up, w_down, ln2_g, ln2_b):
    depth = w_in.shape[0]
    b = x_prompt.shape[0]
    wts = _prep_weights(w_in, w_branch, w_o, w_up, w_down)
    shared = (wts, conv_w, fox_fbias, diff_lambda, diff_gain, ln1_g, ln1_b, ln2_g, ln2_b)
    conv0 = jnp.zeros((depth, b, CONV_W - 1, BR_W), x_prompt.dtype)
    y_p, conv_p, dsa_p, fox_p, diff_p = _trunk(x_prompt, 0, conv0, None, None, None, *shared)
    y_s, conv_s, dsa_s, fox_s, diff_s = _trunk(x_sample, cache_dsa.shape[2], state_conv, cache_dsa,
                                               cache_fox, cache_diff, *shared)
    return (y_p, y_s, conv_p, dsa_p, fox_p, diff_p, conv_s, dsa_s, fox_s, diff_s)
```
